```python
import math
import jax, jax.numpy as jnp
from jax import lax
import numpy as np

D_MODEL = 1024
BATCH = 32
SEQ = 2048
DEPTH = 1
DEC_BATCH = 128
DEC_SEQ = 4
PAST_LEN = 8192
PAGE_SIZE = 128

POOL_WINDOWS = (2, 4, 8, 16)
N_POOL_GROUPS = len(POOL_WINDOWS)
POOL_WIDTH = D_MODEL // 2
POOL_GROUP = POOL_WIDTH // N_POOL_GROUPS
POOL_HIST = max(POOL_WINDOWS) - 1

ATTN_GROUPS = ((128, 1), (512, 4), (2048, 16))
N_ATTN_GROUPS = len(ATTN_GROUPS)
HEAD_DIM = 64
HEADS_PER_GROUP = D_MODEL // 256
GROUP_WIDTH = HEADS_PER_GROUP * HEAD_DIM
ROPE_THETA = 10000.0
QB = 128

D_FF = 11 * D_MODEL // 4
CONV_W = 3

RMS_EPS = 1e-6
GATE_OFF = POOL_WIDTH + 3 * N_ATTN_GROUPS * GROUP_WIDTH
IN_WIDTH = GATE_OFF + 2 * D_MODEL

kernel_name = 'hybrid_pool_dilated_attn_convffn_step'


def rmsnorm(x, g):
    xf = x.astype(jnp.float32)
    y = xf * lax.rsqrt(jnp.mean(xf * xf, axis=-1, keepdims=True) + RMS_EPS)
    return (y * g.astype(jnp.float32)).astype(x.dtype)


def rope(x, pos):
    dh = x.shape[-1]
    inv_freq = ROPE_THETA ** (-jnp.arange(0, dh, 2, dtype=jnp.float32) / dh)
    ang = pos.astype(jnp.float32)[:, None] * inv_freq[None, :]
    cos = jnp.cos(ang)[None, :, None, :]
    sin = jnp.sin(ang)[None, :, None, :]
    xf = x.astype(jnp.float32)
    x1, x2 = xf[..., : dh // 2], xf[..., dh // 2:]
    return jnp.concatenate([x1 * cos - x2 * sin, x2 * cos + x1 * sin], axis=-1).astype(x.dtype)


def pool_mixer(u, hist, pos, pool_w, pool_scale):
    B, T, P = u.shape
    ext = jnp.concatenate([hist, u], axis=1)
    cs = jnp.cumsum(jnp.pad(ext.astype(jnp.float32), ((0, 0), (1, 0), (0, 0))), axis=1)
    end = cs[:, POOL_HIST + 1:]
    means = []
    for g, w in enumerate(POOL_WINDOWS):
        sl = slice(g * POOL_GROUP, (g + 1) * POOL_GROUP)
        start = cs[:, POOL_HIST + 1 - w: POOL_HIST + 1 - w + T, sl]
        cnt = jnp.minimum(pos + 1, w).astype(jnp.float32)[None, :, None]
        means.append((end[..., sl] - start) / cnt)
    z = (jnp.concatenate(means, axis=-1) - u.astype(jnp.float32)).astype(u.dtype)
    z = z.reshape(B, T, N_POOL_GROUPS, POOL_GROUP)
    y = jnp.einsum('btgc,gcd->btgd', z, pool_w).reshape(B, T, P) * pool_scale
    return y, ext[:, -POOL_HIST:]


def dilated_attn_prompt(q, k, v, window, dil):
    B, S, H, Dh = q.shape
    M = S // dil
    kmax = window // dil
    nb = -(-M // QB)
    Mp = nb * QB
    scale = 1.0 / math.sqrt(Dh)

    def to_blocks(a):
        a = a.reshape(B, M, dil, H, Dh).transpose(0, 2, 3, 1, 4)
        a = jnp.pad(a, ((0, 0), (0, 0), (0, 0), (0, Mp - M), (0, 0)))
        return a.reshape(B, dil, H, nb, QB, Dh)

    def with_prev(a):
        prev = jnp.pad(a, ((0, 0), (0, 0), (0, 0), (1, 0), (0, 0), (0, 0)))[:, :, :, :nb]
        return jnp.concatenate([prev, a], axis=4)

    qb = to_blocks(q)
    kk = with_prev(to_blocks(k))
    vv = with_prev(to_blocks(v))
    s = jnp.einsum('bdhnqc,bdhnkc->bdhnqk', qb, kk).astype(jnp.float32) * scale
    qi = jnp.arange(QB)[:, None] + QB
    ki = jnp.arange(2 * QB)[None, :]
    diff = qi - ki
    blk = jnp.arange(nb)[:, None, None]
    valid = (diff >= 0) & (diff <= kmax) & ((blk > 0) | (ki >= QB))
    s = jnp.where(valid, s, -jnp.inf)
    lse = jax.nn.logsumexp(s, axis=-1)
    p = jnp.exp(s - lse[..., None])
    o = jnp.einsum('bdhnqk,bdhnkc->bdhnqc', p.astype(vv.dtype), vv)
    o = o.reshape(B, dil, H, Mp, Dh)[:, :, :, :M].transpose(0, 3, 1, 2, 4).reshape(B, S, H, Dh)
    lse = lse.reshape(B, dil, H, Mp)[..., :M].transpose(0, 3, 1, 2).reshape(B, S, H)
    return o, lse


def dilated_attn_sample(q, k_new, v_new, k_buf, v_buf, window, dil):
    B, T = q.shape[:2]
    Wb = k_buf.shape[1]
    kmax = window // dil
    scale = 1.0 / math.sqrt(q.shape[-1])
    kk = jnp.concatenate([k_buf, k_new], axis=1)
    vv = jnp.concatenate([v_buf, v_new], axis=1)
    idx = Wb + jnp.arange(T)[:, None] - dil * jnp.arange(kmax + 1)[None, :]
    valid = idx >= 0
    idx = jnp.maximum(idx, 0)
    kg = kk[:, idx]
    vg = vv[:, idx]
    s = jnp.einsum('bthc,btkhc->bhtk', q, kg).astype(jnp.float32) * scale
    s = jnp.where(valid[None, None], s, -jnp.inf)
    lse = jax.nn.logsumexp(s, axis=-1)
    p = jnp.exp(s - lse[..., None])
    o = jnp.einsum('bhtk,btkhc->bthc', p.astype(vg.dtype), vg)
    return o, lse.transpose(0, 2, 1), kk[:, -Wb:], vv[:, -Wb:]


def trunk_layer(x, pos, pool_hist, conv_hist, kv_bufs, g_mix_pre, g_mix_post, g_ffn_pre, g_ffn_post,
                w_in, b_gate, pool_w, pool_scale, w_branch_pool, w_branch_attn, w_out,
                ffn_w_up, ffn_conv_w, ffn_conv_b, ffn_w_down):
    B, T, _ = x.shape
    h = rmsnorm(x, g_mix_pre)
    z = h @ w_in
    pool_out, new_pool = pool_mixer(z[..., :POOL_WIDTH], pool_hist, pos, pool_w, pool_scale)
    outs, lses, new_kv = [], [], []
    for g, (window, dil) in enumerate(ATTN_GROUPS):
        base = POOL_WIDTH + 3 * g * GROUP_WIDTH
        q = z[..., base: base + GROUP_WIDTH].reshape(B, T, HEADS_PER_GROUP, HEAD_DIM)
        k = z[..., base + GROUP_WIDTH: base + 2 * GROUP_WIDTH].reshape(B, T, HEADS_PER_GROUP, HEAD_DIM)
        v = z[..., base + 2 * GROUP_WIDTH: base + 3 * GROUP_WIDTH].reshape(B, T, HEADS_PER_GROUP, HEAD_DIM)
        q = rope(q, pos)
        k = rope(k, pos)
        if kv_bufs is None:
            o, lse = dilated_attn_prompt(q, k, v, window, dil)
            keep = min(window, T)
            new_kv.append(k[:, T - keep:])
            new_kv.append(v[:, T - keep:])
        else:
            o, lse, nk, nv = dilated_attn_sample(q, k, v, kv_bufs[g][0], kv_bufs[g][1], window, dil)
            new_kv.append(nk)
            new_kv.append(nv)
        outs.append(o)
        lses.append(lse)
    wts = jax.nn.softmax(jnp.stack(lses, axis=0), axis=0)
    attn = jnp.einsum('gbth,gbthc->bthc', wts, jnp.stack(outs, axis=0).astype(jnp.float32))
    attn = attn.astype(x.dtype).reshape(B, T, GROUP_WIDTH)
    gates = jax.nn.sigmoid((z[..., GATE_OFF:] + b_gate).astype(jnp.float32))
    merged = (gates[..., :D_MODEL] * (pool_out @ w_branch_pool).astype(jnp.float32)
              + gates[..., D_MODEL:] * (attn @ w_branch_attn).astype(jnp.float32)).astype(x.dtype)
    x = x + rmsnorm(merged @ w_out, g_mix_post)
    up = rmsnorm(x, g_ffn_pre) @ ffn_w_up
    ext = jnp.concatenate([conv_hist, up], axis=1)
    c = ffn_conv_b
    for j in range(CONV_W):
        c = c + ext[:, j: j + T] * ffn_conv_w[j]
    act = jax.nn.gelu(c[..., :D_FF], approximate=True) * c[..., D_FF:]
    x = x + rmsnorm(act @ ffn_w_down, g_ffn_post)
    states = (new_pool, new_kv[0], new_kv[1], new_kv[2], new_kv[3], new_kv[4], new_kv[5], ext[:, -(CONV_W - 1):])
    return x, states


def setup_inputs(seed: int = 0) -> dict:
    key = jax.random.key(seed)
    ks = iter(jax.random.split(key, 32))

    def nrm(shape, scale):
        return scale * jax.random.normal(next(ks), shape, jnp.float32)

    wb = [min(w, PAST_LEN) for w, _ in ATTN_GROUPS]
    kvs = (DEPTH, DEC_BATCH)
    hd = (HEADS_PER_GROUP, HEAD_DIM)
    return {
        'x_prompt': nrm((BATCH, SEQ, D_MODEL), 1.0),
        'x_sample': nrm((DEC_BATCH, DEC_SEQ, D_MODEL), 1.0),
        'state_pool': nrm((DEPTH, DEC_BATCH, POOL_HIST, POOL_WIDTH), 1.0),
        'cache_k_w128': nrm(kvs + (wb[0],) + hd, 1.0),
        'cache_v_w128': nrm(kvs + (wb[0],) + hd, 1.0),
        'cache_k_w512': nrm(kvs + (wb[1],) + hd, 1.0),
        'cache_v_w512': nrm(kvs + (wb[1],) + hd, 1.0),
        'cache_k_w2048': nrm(kvs + (wb[2],) + hd, 1.0),
        'cache_v_w2048': nrm(kvs + (wb[2],) + hd, 1.0),
        'state_ffn_conv': nrm((DEPTH, DEC_BATCH, CONV_W - 1, 2 * D_FF), 1.0),
        'norm_mix_pre': 1.0 + nrm((DEPTH, D_MODEL), 0.05),
        'norm_mix_post': 1.0 + nrm((DEPTH, D_MODEL), 0.05),
        'norm_ffn_pre': 1.0 + nrm((DEPTH, D_MODEL), 0.05),
        'norm_ffn_post': 1.0 + nrm((DEPTH, D_MODEL), 0.05),
        'w_in': nrm((DEPTH, D_MODEL, IN_WIDTH), D_MODEL ** -0.5),
        'b_gate': nrm((DEPTH, 2 * D_MODEL), 0.1),
        'pool_w': nrm((DEPTH, N_POOL_GROUPS, POOL_GROUP, POOL_GROUP), POOL_GROUP ** -0.5),
        'pool_scale': 1.0 + nrm((DEPTH, POOL_WIDTH), 0.1),
        'w_branch_pool': nrm((DEPTH, POOL_WIDTH, D_MODEL), POOL_WIDTH ** -0.5),
        'w_branch_attn': nrm((DEPTH, GROUP_WIDTH, D_MODEL), GROUP_WIDTH ** -0.5),
        'w_out': nrm((DEPTH, D_MODEL, D_MODEL), D_MODEL ** -0.5),
        'ffn_w_up': nrm((DEPTH, D_MODEL, 2 * D_FF), D_MODEL ** -0.5),
        'ffn_conv_w': nrm((DEPTH, CONV_W, 2 * D_FF), CONV_W ** -0.5),
        'ffn_conv_b': nrm((DEPTH, 2 * D_FF), 0.02),
        'ffn_w_down': nrm((DEPTH, D_FF, D_MODEL), D_FF ** -0.5),
    }


def reference(x_prompt, x_sample, state_pool, cache_k_w128, cache_v_w128, cache_k_w512, cache_v_w512,
              cache_k_w2048, cache_v_w2048, state_ffn_conv, norm_mix_pre, norm_mix_post, norm_ffn_pre,
              norm_ffn_post, w_in, b_gate, pool_w, pool_scale, w_branch_pool, w_branch_attn, w_out,
              ffn_w_up, ffn_conv_w, ffn_conv_b, ffn_w_down):
    pos_p = jnp.arange(SEQ, dtype=jnp.int32)
    pos_s = PAST_LEN + jnp.arange(DEC_SEQ, dtype=jnp.int32)
    yp, ys = x_prompt, x_sample
    new_p, new_s = [], []
    for l in range(DEPTH):
        wl = (norm_mix_pre[l], norm_mix_post[l], norm_ffn_pre[l], norm_ffn_post[l], w_in[l], b_gate[l],
              pool_w[l], pool_scale[l], w_branch_pool[l], w_branch_attn[l], w_out[l],
              ffn_w_up[l], ffn_conv_w[l], ffn_conv_b[l], ffn_w_down[l])
        pool0 = jnp.zeros((BATCH, POOL_HIST, POOL_WIDTH), x_prompt.dtype)
        conv0 = jnp.zeros((BATCH, CONV_W - 1, 2 * D_FF), x_prompt.dtype)
        yp, sp = trunk_layer(yp, pos_p, pool0, conv0, None, *wl)
        bufs = ((cache_k_w128[l], cache_v_w128[l]), (cache_k_w512[l], cache_v_w512[l]),
                (cache_k_w2048[l], cache_v_w2048[l]))
        ys, ss = trunk_layer(ys, pos_s, state_pool[l], state_ffn_conv[l], bufs, *wl)
        new_p.append(sp)
        new_s.append(ss)
    st_p = [jnp.stack([s[i] for s in new_p], axis=0) for i in range(8)]
    st_s = [jnp.stack([s[i] for s in new_s], axis=0) for i in range(8)]
    return (yp, ys, st_p[0], st_s[0], st_p[1], st_s[1], st_p[2], st_s[2], st_p[3], st_s[3],
            st_p[4], st_s[4], st_p[5], st_s[5], st_p[6], st_s[6], st_p[7], st_s[7])
```

```python
import functools
import math

import jax
import jax.numpy as jnp
from jax import lax
from jax.experimental import pallas as pl
from jax.experimental.pallas import tpu as pltpu

F32 = jnp.float32
BF16 = jnp.bfloat16

D_MODEL = 1024
SEQ = 2048
DEC_SEQ = 4
PAST_LEN = 8192
POOL_WINDOWS = (2, 4, 8, 16)
POOL_WIDTH = 512
POOL_GROUP = 128
POOL_HIST = 15
ATTN_GROUPS = ((128, 1), (512, 4), (2048, 16))
HEAD_DIM = 64
HEADS = 4
GROUP_WIDTH = 256
ROPE_THETA = 10000.0
QB = 128
D_FF = 2816
CONV_W = 3
RMS_EPS = 1e-6
GATE_OFF = POOL_WIDTH + 3 * 3 * GROUP_WIDTH

TS = 512
FC = 256
NCH = D_FF // FC
NEG = -1e30
QK_SCALE = 1.0 / math.sqrt(HEAD_DIM)
VMEM_LIMIT_V7X = 60000 * 1024


def _rms(x, g):
    ms = jnp.mean(x * x, axis=-1, keepdims=True)
    return x * lax.rsqrt(ms + RMS_EPS) * g


def _rope(x, cos, sin_signed):
    lane = lax.broadcasted_iota(jnp.int32, x.shape, 1)
    first_half = (lane % HEAD_DIM) < (HEAD_DIM // 2)
    swapped = jnp.where(first_half,
                        pltpu.roll(x, GROUP_WIDTH - HEAD_DIM // 2, 1),
                        pltpu.roll(x, HEAD_DIM // 2, 1))
    return x * cos + swapped * sin_signed


def _head_expand(cols, lane_head):
    out = jnp.where(lane_head == 0, cols[0], cols[1])
    out = jnp.where(lane_head == 2, cols[2], out)
    return jnp.where(lane_head == 3, cols[3], out)


def _attn_prompt_kernel(x_ref, g_ref, w_ref, cos_ref, sin_ref,
                        o_ref, lse_ref, kout_ref, vout_ref,
                        kbuf, vebuf, vobuf, *, nsub, carry):
    i = pl.program_id(1)
    if nsub == 1:
        x = x_ref[...]
        cos = cos_ref[...]
        sin = sin_ref[...]
    else:
        x = jnp.concatenate([x_ref[:, j * D_MODEL:(j + 1) * D_MODEL] for j in range(nsub)], axis=0)
        cos = jnp.concatenate([cos_ref[:, j * GROUP_WIDTH:(j + 1) * GROUP_WIDTH] for j in range(nsub)], axis=0)
        sin = jnp.concatenate([sin_ref[:, j * GROUP_WIDTH:(j + 1) * GROUP_WIDTH] for j in range(nsub)], axis=0)

    h = _rms(x, g_ref[...]).astype(BF16)
    qkv = jnp.dot(h, w_ref[...], preferred_element_type=F32)
    q = _rope(qkv[:, :GROUP_WIDTH], cos, sin) * QK_SCALE
    k = _rope(qkv[:, GROUP_WIDTH:2 * GROUP_WIDTH], cos, sin)
    v = qkv[:, 2 * GROUP_WIDTH:]

    if nsub == 1:
        kout_ref[...] = k[TS - QB:]
        vout_ref[...] = v[TS - QB:]
    else:
        for j in range(nsub):
            kout_ref[:, j * GROUP_WIDTH:(j + 1) * GROUP_WIDTH] = k[j * QB:(j + 1) * QB]
            vout_ref[:, j * GROUP_WIDTH:(j + 1) * GROUP_WIDTH] = v[j * QB:(j + 1) * QB]

    lane = lax.broadcasted_iota(jnp.int32, (TS, GROUP_WIDTH), 1)
    even_head = (lane % (2 * HEAD_DIM)) < HEAD_DIM
    if carry:
        @pl.when(i == 0)
        def _():
            z = jnp.zeros((QB, GROUP_WIDTH), BF16)
            kbuf[0:QB] = z
            vebuf[0:QB] = z
            vobuf[0:QB] = z

        @pl.when(i > 0)
        def _():
            kbuf[0:QB] = kbuf[TS:TS + QB]
            vebuf[0:QB] = vebuf[TS:TS + QB]
            vobuf[0:QB] = vobuf[TS:TS + QB]
    kbuf[QB:QB + TS] = k.astype(BF16)
    vebuf[QB:QB + TS] = jnp.where(even_head, v, 0.0).astype(BF16)
    vobuf[QB:QB + TS] = jnp.where(even_head, 0.0, v).astype(BF16)

    lane_head_q = lax.broadcasted_iota(jnp.int32, (QB, GROUP_WIDTH), 1) // HEAD_DIM
    qi = lax.broadcasted_iota(jnp.int32, (HEADS * QB, 2 * QB), 0) % QB
    ki2 = lax.broadcasted_iota(jnp.int32, (HEADS * QB, 2 * QB), 1)
    diff = qi + QB - ki2
    band2 = (diff >= 0) & (diff <= QB)
    qi1 = lax.broadcasted_iota(jnp.int32, (HEADS * QB, QB), 0) % QB
    ki1 = lax.broadcasted_iota(jnp.int32, (HEADS * QB, QB), 1)
    band1 = ki1 <= qi1
    lane_o = lax.broadcasted_iota(jnp.int32, (QB, 2 * HEAD_DIM), 1)
    lane_head_o = lax.broadcasted_iota(jnp.int32, (QB, GROUP_WIDTH), 1) // HEAD_DIM

    for blk in range(TS // QB):
        r0 = blk * QB
        has_prev = carry or (nsub == 1 and blk > 0)
        qb = q[r0:r0 + QB]
        qm = jnp.concatenate([jnp.where(lane_head_q == hh, qb, 0.0) for hh in range(HEADS)],
                             axis=0).astype(BF16)
        if has_prev:
            lo, nk = r0, 2 * QB
            valid = band2
            if carry and blk == 0:
                valid = band2 & (ki2 >= jnp.where(i > 0, 0, QB))
        else:
            lo, nk = r0 + QB, QB
            valid = band1
        kw = kbuf[lo:lo + nk]
        s = lax.dot_general(qm, kw, (((1,), (1,)), ((), ())), preferred_element_type=F32)
        s = jnp.where(valid, s, NEG)
        m = jnp.max(s, axis=-1, keepdims=True)
        p = jnp.exp(s - m)
        l = jnp.sum(p, axis=-1, keepdims=True)
        p16 = p.astype(BF16)
        vew = vebuf[lo:lo + nk]
        vow = vobuf[lo:lo + nk]
        inv_l = 1.0 / l
        halves = []
        for pr in range(2):
            c0 = pr * 2 * HEAD_DIM
            acc = (jnp.dot(p16[(2 * pr) * QB:(2 * pr + 1) * QB], vew[:, c0:c0 + 2 * HEAD_DIM],
                           preferred_element_type=F32)
                   + jnp.dot(p16[(2 * pr + 1) * QB:(2 * pr + 2) * QB], vow[:, c0:c0 + 2 * HEAD_DIM],
                             preferred_element_type=F32))
            scale = jnp.where(lane_o < HEAD_DIM, inv_l[(2 * pr) * QB:(2 * pr + 1) * QB],
                              inv_l[(2 * pr + 1) * QB:(2 * pr + 2) * QB])
            halves.append(acc * scale)
        o_blk = jnp.concatenate(halves, axis=1)
        lse = m + jnp.log(l)
        lse_blk = _head_expand([lse[hh * QB:(hh + 1) * QB] for hh in range(HEADS)], lane_head_o)
        if nsub == 1:
            o_ref[r0:r0 + QB, :] = o_blk
            lse_ref[r0:r0 + QB, :] = lse_blk
        else:
            o_ref[:, blk * GROUP_WIDTH:(blk + 1) * GROUP_WIDTH] = o_blk
            lse_ref[:, blk * GROUP_WIDTH:(blk + 1) * GROUP_WIDTH] = lse_blk


def _attn_prompt(x, g_pre, w_qkv, cos_t, sin_t, dil):
    B = x.shape[0]
    M = SEQ // dil
    nsub = TS // M if M < TS else 1
    carry = M > TS
    steps = SEQ // TS
    keep = min(ATTN_GROUPS[{1: 0, 4: 1, 16: 2}[dil]][0], SEQ)
    if dil == 1:
        xv, tv = x, cos_t
        x_spec = pl.BlockSpec((None, TS, D_MODEL), lambda b, i: (b, i, 0))
        t_spec = pl.BlockSpec((TS, GROUP_WIDTH), lambda b, i: (i, 0))
        o_spec = pl.BlockSpec((None, TS, GROUP_WIDTH), lambda b, i: (b, i, 0))
        o_shape = (B, SEQ, GROUP_WIDTH)
        kv_spec = pl.BlockSpec((None, QB, GROUP_WIDTH), lambda b, i: (b, 0, 0))
        kv_shape = (B, QB, GROUP_WIDTH)
    else:
        rows = M if nsub == 1 else QB
        wide = nsub
        xv = x.reshape(B, M, dil * D_MODEL)
        x_spec = pl.BlockSpec((None, rows, wide * D_MODEL), lambda b, i: (b, 0, i))
        t_spec = pl.BlockSpec((rows, wide * GROUP_WIDTH), lambda b, i: (0, i))
        o_spec = pl.BlockSpec((None, rows, wide * GROUP_WIDTH), lambda b, i: (b, 0, i))
        o_shape = (B, M, dil * GROUP_WIDTH)
        kv_spec = pl.BlockSpec((None, QB, wide * GROUP_WIDTH), lambda b, i: (b, 0, i))
        kv_shape = (B, QB, dil * GROUP_WIDTH)
    cos_v = cos_t.reshape(M, dil * GROUP_WIDTH)
    sin_v = sin_t.reshape(M, dil * GROUP_WIDTH)
    const = lambda b, i: (0, 0)
    o, lse, ko, vo = pl.pallas_call(
        functools.partial(_attn_prompt_kernel, nsub=nsub, carry=carry),
        grid=(B, steps),
        in_specs=[x_spec,
                  pl.BlockSpec((1, D_MODEL), const),
                  pl.BlockSpec((D_MODEL, 3 * GROUP_WIDTH), const),
                  t_spec, t_spec],
        out_specs=[o_spec, o_spec, kv_spec, kv_spec],
        out_shape=[jax.ShapeDtypeStruct(o_shape, F32), jax.ShapeDtypeStruct(o_shape, F32),
                   jax.ShapeDtypeStruct(kv_shape, F32), jax.ShapeDtypeStruct(kv_shape, F32)],
        scratch_shapes=[pltpu.VMEM((QB + TS, GROUP_WIDTH), BF16)] * 3,
        compiler_params=pltpu.CompilerParams(dimension_semantics=("arbitrary", "arbitrary"),
                                             vmem_limit_bytes=VMEM_LIMIT_V7X),
        name=f"attn_prompt_d{dil}",
    )(xv, g_pre, w_qkv, cos_v, sin_v)
    o = o.reshape(B, SEQ, GROUP_WIDTH)
    lse = lse.reshape(B, SEQ, GROUP_WIDTH)
    ko = ko.reshape(1, B, keep, HEADS, HEAD_DIM)
    vo = vo.reshape(1, B, keep, HEADS, HEAD_DIM)
    return o, lse, ko, vo


def _dense_body(x, o_refs, l_refs, wr, *, keep_u, u_back, cnt_of, keep_up, up_back):
    (gpre, gpost, gfpre, gfpost, wpool, wgate, bgate, poolw, pscale, wbp, wba, wout,
     wupa, wupb, cwa, cwb, cba, cbb, wdown) = wr
    R = x.shape[0]
    h = _rms(x, gpre[...]).astype(BF16)

    u = jnp.dot(h, wpool[...], preferred_element_type=F32)
    keep_u(u)
    ys = []
    for gi, w in enumerate(POOL_WINDOWS):
        c0 = gi * POOL_GROUP
        ssum = u[:, c0:c0 + POOL_GROUP]
        for j in range(1, w):
            ssum = ssum + u_back(j, c0)
        zmix = ssum / cnt_of(w) - u[:, c0:c0 + POOL_GROUP]
        ys.append(jnp.dot(zmix.astype(BF16), poolw[gi], preferred_element_type=F32))
    pool_out = jnp.concatenate(ys, axis=1) * pscale[...]
    a_br = jnp.dot(pool_out.astype(BF16), wbp[...], preferred_element_type=F32)

    l0, l1, l2 = l_refs[0][...], l_refs[1][...], l_refs[2][...]
    mx = jnp.maximum(jnp.maximum(l0, l1), l2)
    e0, e1, e2 = jnp.exp(l0 - mx), jnp.exp(l1 - mx), jnp.exp(l2 - mx)
    attn = (e0 * o_refs[0][...] + e1 * o_refs[1][...] + e2 * o_refs[2][...]) / (e0 + e1 + e2)
    b_br = jnp.dot(attn.astype(BF16), wba[...], preferred_element_type=F32)

    gates = jax.nn.sigmoid(jnp.dot(h, wgate[...], preferred_element_type=F32) + bgate[...])
    merged = gates[:, :D_MODEL] * a_br + gates[:, D_MODEL:] * b_br
    mix = jnp.dot(merged.astype(BF16), wout[...], preferred_element_type=F32)
    x1 = x + _rms(mix, gpost[...])

    hn = _rms(x1, gfpre[...]).astype(BF16)
    acc = jnp.zeros((R, D_MODEL), F32)
    for c in range(NCH):
        ua = jnp.dot(hn, wupa[c], preferred_element_type=F32)
        ub = jnp.dot(hn, wupb[c], preferred_element_type=F32)
        keep_up(c, ua, ub)
        wa = cwa[c]
        wb = cwb[c]
        ca = cba[c] + ua * wa[CONV_W - 1:CONV_W]
        cb = cbb[c] + ub * wb[CONV_W - 1:CONV_W]
        for j in range(1, CONV_W):
            ca = ca + up_back(0, j) * wa[CONV_W - 1 - j:CONV_W - j]
            cb = cb + up_back(1, j) * wb[CONV_W - 1 - j:CONV_W - j]
        act = jax.nn.gelu(ca, approximate=True) * cb
        acc = acc + jnp.dot(act.astype(BF16), wdown[c], preferred_element_type=F32)
    return x1 + _rms(acc, gfpost[...])


N_W = 19
HU_ROWS = 16
HC_ROWS = 8


def _dense_prompt_kernel(*refs):
    x_ref = refs[0]
    o_refs, l_refs = refs[1:4], refs[4:7]
    wr = refs[7:7 + N_W]
    y_ref, pool_ref, conv_ref = refs[7 + N_W:10 + N_W]
    ubuf, cabuf, cbbuf, chist = refs[10 + N_W:]
    i = pl.program_id(1)
    HU, HC = HU_ROWS, HC_ROWS

    @pl.when(i == 0)
    def _():
        ubuf[0:HU] = jnp.zeros((HU, POOL_WIDTH), F32)
        chist[...] = jnp.zeros(chist.shape, F32)

    @pl.when(i > 0)
    def _():
        ubuf[0:HU] = ubuf[TS:TS + HU]

    pos = i * TS + lax.broadcasted_iota(jnp.int32, (TS, 1), 0)

    def keep_u(u):
        ubuf[HU:HU + TS] = u
        pool_ref[...] = ubuf[HU + TS - POOL_HIST:HU + TS]

    def u_back(j, c0):
        return ubuf[HU - j:HU - j + TS, c0:c0 + POOL_GROUP]

    def cnt_of(w):
        return jnp.minimum(pos + 1, w).astype(F32)

    def keep_up(c, ua, ub):
        for half, (buf, val) in enumerate(((cabuf, ua), (cbbuf, ub))):
            buf[0:HC] = chist[2 * c + half]
            buf[HC:HC + TS] = val
            chist[2 * c + half] = buf[TS:TS + HC]
            conv_ref[:, half * D_FF + c * FC:half * D_FF + (c + 1) * FC] = buf[HC + TS - (CONV_W - 1):HC + TS]

    def up_back(half, j):
        buf = (cabuf, cbbuf)[half]
        return buf[HC - j:HC - j + TS]

    y_ref[...] = _dense_body(x_ref[...], o_refs, l_refs, wr, keep_u=keep_u, u_back=u_back, cnt_of=cnt_of,
                             keep_up=keep_up, up_back=up_back)


def _dense_sample_kernel(*refs, bs):
    x_ref = refs[0]
    o_refs, l_refs = refs[1:4], refs[4:7]
    ph_ref, ch_ref = refs[7:9]
    wr = refs[9:9 + N_W]
    y_ref, u_ref, conv_ref = refs[9 + N_W:12 + N_W]
    R = DEC_SEQ * bs
    HU = POOL_HIST * bs
    HC = (CONV_W - 1) * bs
    state = {}

    def keep_u(u):
        state["u"] = u
        u_ref[...] = u

    def u_back(j, c0):
        lanes = slice(c0, c0 + POOL_GROUP)
        if j >= DEC_SEQ:
            return ph_ref[HU - j * bs:HU - j * bs + R, lanes]
        return jnp.concatenate([ph_ref[HU - j * bs:HU, lanes], state["u"][0:R - j * bs, lanes]], axis=0)

    def cnt_of(w):
        return float(w)

    def keep_up(c, ua, ub):
        state["c"], state["up"] = c, (ua, ub)
        conv_ref[:, c * FC:(c + 1) * FC] = ua[R - HC:R]
        conv_ref[:, D_FF + c * FC:D_FF + (c + 1) * FC] = ub[R - HC:R]

    def up_back(half, j):
        c0 = half * D_FF + state["c"] * FC
        return jnp.concatenate([ch_ref[HC - j * bs:HC, c0:c0 + FC], state["up"][half][0:R - j * bs]], axis=0)

    y_ref[...] = _dense_body(x_ref[...], o_refs, l_refs, wr, keep_u=keep_u, u_back=u_back, cnt_of=cnt_of,
                             keep_up=keep_up, up_back=up_back)


def _weight_shapes():
    return [
        (1, D_MODEL), (1, D_MODEL), (1, D_MODEL), (1, D_MODEL),
        (D_MODEL, POOL_WIDTH), (D_MODEL, 2 * D_MODEL), (1, 2 * D_MODEL),
        (len(POOL_WINDOWS), POOL_GROUP, POOL_GROUP), (1, POOL_WIDTH),
        (POOL_WIDTH, D_MODEL), (GROUP_WIDTH, D_MODEL), (D_MODEL, D_MODEL),
        (NCH, D_MODEL, FC), (NCH, D_MODEL, FC),
        (NCH, CONV_W, FC), (NCH, CONV_W, FC),
        (NCH, 1, FC), (NCH, 1, FC),
        (NCH, FC, D_MODEL),
    ]


def _dense_prompt(x, os_, ls_, weights):
    B = x.shape[0]
    steps = SEQ // TS
    tile = lambda w: pl.BlockSpec((None, TS, w), lambda b, i: (b, i, 0))
    wspecs = [pl.BlockSpec(s, lambda b, i, n=len(s): (0,) * n, pipeline_mode=pl.Buffered(1)) for s in _weight_shapes()]
    y, pool_p, conv_p = pl.pallas_call(
        _dense_prompt_kernel,
        grid=(B, steps),
        in_specs=[tile(D_MODEL)] + [tile(GROUP_WIDTH)] * 6 + wspecs,
        out_specs=[tile(D_MODEL),
                   pl.BlockSpec((None, POOL_HIST, POOL_WIDTH), lambda b, i: (b, 0, 0)),
                   pl.BlockSpec((None, CONV_W - 1, 2 * D_FF), lambda b, i: (b, 0, 0))],
        out_shape=[jax.ShapeDtypeStruct((B, SEQ, D_MODEL), F32),
                   jax.ShapeDtypeStruct((B, POOL_HIST, POOL_WIDTH), F32),
                   jax.ShapeDtypeStruct((B, CONV_W - 1, 2 * D_FF), F32)],
        scratch_shapes=[pltpu.VMEM((HU_ROWS + TS, POOL_WIDTH), F32),
                        pltpu.VMEM((HC_ROWS + TS, FC), F32),
                        pltpu.VMEM((HC_ROWS + TS, FC), F32),
                        pltpu.VMEM((2 * NCH, HC_ROWS, FC), F32)],
        compiler_params=pltpu.CompilerParams(dimension_semantics=("arbitrary", "arbitrary"),
                                             vmem_limit_bytes=VMEM_LIMIT_V7X),
        name="dense_prompt",
    )(x, *os_, *ls_, *weights)
    return y, pool_p, conv_p


def _dense_sample(x_tm, os_, ls_, pool_hist_tm, conv_hist_tm, weights, bs):
    R = DEC_SEQ * bs
    full = lambda shape: pl.BlockSpec(shape, lambda n=len(shape): (0,) * n)
    y, u, conv_s = pl.pallas_call(
        functools.partial(_dense_sample_kernel, bs=bs),
        in_specs=[full((R, D_MODEL))] + [full((R, GROUP_WIDTH))] * 6
                 + [full((POOL_HIST * bs, POOL_WIDTH)), full(((CONV_W - 1) * bs, 2 * D_FF))]
                 + [full(s) for s in _weight_shapes()],
        out_specs=[full((R, D_MODEL)), full((R, POOL_WIDTH)), full(((CONV_W - 1) * bs, 2 * D_FF))],
        out_shape=[jax.ShapeDtypeStruct((R, D_MODEL), F32),
                   jax.ShapeDtypeStruct((R, POOL_WIDTH), F32),
                   jax.ShapeDtypeStruct(((CONV_W - 1) * bs, 2 * D_FF), F32)],
        compiler_params=pltpu.CompilerParams(vmem_limit_bytes=VMEM_LIMIT_V7X),
        name="dense_sample",
    )(x_tm, *os_, *ls_, pool_hist_tm, conv_hist_tm, *weights)
    return y, u, conv_s


def _attn_sample_kernel(x_ref, g_ref, w_ref, cos_ref, sin_ref,
                        k0_ref, v0_ref, k1_ref, v1_ref, k2_ref, v2_ref,
                        o0_ref, o1_ref, o2_ref, l0_ref, l1_ref, l2_ref,
                        kn0_ref, vn0_ref, kn1_ref, vn1_ref, kn2_ref, vn2_ref,
                        qkv_s, sn_s, *, bs, bb):
    bi = pl.program_id(0)
    T = DEC_SEQ
    W = GROUP_WIDTH
    row_h = lax.broadcasted_iota(jnp.int32, (W, W), 0) // HEAD_DIM
    col_h = lax.broadcasted_iota(jnp.int32, (W, W), 1) // HEAD_DIM
    seg = jnp.where(row_h == col_h, 1.0, 0.0).astype(BF16)

    def head_sum(prod):
        return jnp.dot(prod.astype(BF16), seg, preferred_element_type=F32)

    slots = {}
    for g in range(3):
        for t in range(T):
            for tp in (range(t + 1) if g == 0 else (t,)):
                slots[(g, t, tp)] = len(slots)

    @pl.when(bi == 0)
    def _():
        h = _rms(x_ref[...], g_ref[...]).astype(BF16)
        qkv = jnp.dot(h, w_ref[...], preferred_element_type=F32)
        cos = cos_ref[...]
        sin = sin_ref[...]
        for g in range(3):
            b0 = 3 * W * g
            q = _rope(qkv[:, b0:b0 + W], cos, sin) * QK_SCALE
            k = _rope(qkv[:, b0 + W:b0 + 2 * W], cos, sin)
            qkv_s[:, b0:b0 + W] = q
            qkv_s[:, b0 + W:b0 + 2 * W] = k
            qkv_s[:, b0 + 2 * W:b0 + 3 * W] = qkv[:, b0 + 2 * W:b0 + 3 * W]
            for t in range(T):
                for tp in (range(t + 1) if g == 0 else (t,)):
                    sn_s[slots[(g, t, tp)]] = head_sum(q[t * bs:(t + 1) * bs] * k[tp * bs:(tp + 1) * bs])

    krefs = (k0_ref, k1_ref, k2_ref)
    vrefs = (v0_ref, v1_ref, v2_ref)
    orefs = (o0_ref, o1_ref, o2_ref)
    lrefs = (l0_ref, l1_ref, l2_ref)
    knrefs = (kn0_ref, kn1_ref, kn2_ref)
    vnrefs = (vn0_ref, vn1_ref, vn2_ref)
    key_row = lax.broadcasted_iota(jnp.int32, (QB, W), 0)

    def per_b(j, carry_):
        b = bi * bb + j
        for g in range(3):
            b0 = 3 * W * g
            for t in range(T):
                row = t * bs + b
                qrow = qkv_s[pl.ds(row, 1), b0:b0 + W]
                knrefs[g][pl.ds(b * T + t, 1), :] = qkv_s[pl.ds(row, 1), b0 + W:b0 + 2 * W]
                knrefs_v = qkv_s[pl.ds(row, 1), b0 + 2 * W:b0 + 3 * W]
                vnrefs[g][pl.ds(b * T + t, 1), :] = knrefs_v
                if g == 0:
                    kc = krefs[g][j]
                    vc = vrefs[g][j]
                else:
                    kc = krefs[g][j, :, t * W:(t + 1) * W]
                    vc = vrefs[g][j, :, t * W:(t + 1) * W]
                sc = head_sum(kc * qrow)
                if g == 0:
                    sc = jnp.where(key_row >= t, sc, NEG)
                tps = tuple(range(t + 1)) if g == 0 else (t,)
                sn = [sn_s[slots[(g, t, tp)], pl.ds(b, 1), :] for tp in tps]
                m = jnp.max(sc, axis=0, keepdims=True)
                for s1 in sn:
                    m = jnp.maximum(m, s1)
                p = jnp.exp(sc - m)
                l = jnp.sum(p, axis=0, keepdims=True)
                acc = jnp.sum(p * vc, axis=0, keepdims=True)
                for tp, s1 in zip(tps, sn):
                    pn = jnp.exp(s1 - m)
                    l = l + pn
                    acc = acc + pn * qkv_s[pl.ds(tp * bs + b, 1), b0 + 2 * W:b0 + 3 * W]
                orefs[g][pl.ds(row, 1), :] = acc / l
                lrefs[g][pl.ds(row, 1), :] = m + jnp.log(l)
        return carry_

    lax.fori_loop(0, bb, per_b, 0)


def _attn_sample(x_tm, g_pre, w_qkv_all, cos_s, sin_s, caches, bs, bb):
    R = DEC_SEQ * bs
    W = GROUP_WIDTH
    views, cspecs = [], []
    for idx, c in enumerate(caches):
        dil = ATTN_GROUPS[idx // 2][1]
        wb = c.shape[1]
        views.append(c.reshape(bs, wb // dil, dil * W))
        width = W if dil == 1 else DEC_SEQ * W
        cspecs.append(pl.BlockSpec((bb, QB, width), lambda i: (i, 0, 0)))
    full = lambda shape: pl.BlockSpec(shape, lambda i: (0,) * len(shape))
    n_slots = sum(t + 1 for t in range(DEC_SEQ)) + 2 * DEC_SEQ
    outs = pl.pallas_call(
        functools.partial(_attn_sample_kernel, bs=bs, bb=bb),
        grid=(bs // bb,),
        in_specs=[full((R, D_MODEL)), full((1, D_MODEL)), full((D_MODEL, 9 * W)), full((R, W)), full((R, W))] + cspecs,
        out_specs=[full((R, W))] * 12,
        out_shape=[jax.ShapeDtypeStruct((R, W), F32)] * 12,
        scratch_shapes=[pltpu.VMEM((R, 9 * W), F32), pltpu.VMEM((n_slots, bs, W), F32)],
        compiler_params=pltpu.CompilerParams(dimension_semantics=("arbitrary",),
                                             vmem_limit_bytes=VMEM_LIMIT_V7X),
        name="attn_sample",
    )(x_tm, g_pre, w_qkv_all, cos_s, sin_s, *views)
    return outs[0:3], outs[3:6], outs[6:12]


LANES = 128
ROLL_ROWS = 512


def _cache_roll_kernel(*refs):
    n = len(refs) // 3
    caches, news, outs = refs[:n], refs[n:2 * n], refs[2 * n:]
    lane = lax.broadcasted_iota(jnp.int32, (ROLL_ROWS, LANES), 1)
    for c_ref, n_ref, o_ref in zip(caches, news, outs):
        wb = c_ref.shape[-1]
        rolled = pltpu.roll(c_ref[...], wb - DEC_SEQ, 1)
        if wb > LANES:
            o_ref[:, :wb - LANES] = rolled[:, :wb - LANES]
        o_ref[:, wb - LANES:] = jnp.where(lane >= LANES - DEC_SEQ, n_ref[...], rolled[:, wb - LANES:])


def _cache_roll(caches_t, news_t):
    rows = caches_t[0].shape[0]
    spec = lambda wb: pl.BlockSpec((ROLL_ROWS, wb), lambda i: (i, 0))
    cspecs = [spec(c.shape[1]) for c in caches_t]
    return pl.pallas_call(
        _cache_roll_kernel,
        grid=(rows // ROLL_ROWS,),
        in_specs=cspecs + [spec(LANES)] * len(caches_t),
        out_specs=cspecs,
        out_shape=[jax.ShapeDtypeStruct(c.shape, c.dtype) for c in caches_t],
        compiler_params=pltpu.CompilerParams(dimension_semantics=("arbitrary",),
                                             vmem_limit_bytes=VMEM_LIMIT_V7X),
        name="cache_roll",
    )(*caches_t, *news_t)


def _rope_tables(pos):
    inv_freq = ROPE_THETA ** (-jnp.arange(0, HEAD_DIM, 2, dtype=F32) / HEAD_DIM)
    ang = pos.astype(F32)[:, None] * inv_freq[None, :]
    cos = jnp.cos(ang)
    sin = jnp.sin(ang)
    cos_t = jnp.tile(jnp.concatenate([cos, cos], axis=-1), (1, HEADS))
    sin_t = jnp.tile(jnp.concatenate([-sin, sin], axis=-1), (1, HEADS))
    return cos_t, sin_t


def _layer_weights(l, norm_mix_pre, norm_mix_post, norm_ffn_pre, norm_ffn_post, w_in, b_gate, pool_w, pool_scale,
                   w_branch_pool, w_branch_attn, w_out, ffn_w_up, ffn_conv_w, ffn_conv_b, ffn_w_down):
    row = lambda a: a[l].reshape(1, -1)
    chunk_cols = lambda a: a.reshape(a.shape[0], NCH, FC).transpose(1, 0, 2)
    wi = w_in[l]
    up = ffn_w_up[l]
    cw = ffn_conv_w[l]
    cb = ffn_conv_b[l]
    dense = (
        row(norm_mix_pre), row(norm_mix_post), row(norm_ffn_pre), row(norm_ffn_post),
        wi[:, :POOL_WIDTH].astype(BF16), wi[:, GATE_OFF:].astype(BF16), row(b_gate),
        pool_w[l].astype(BF16), row(pool_scale),
        w_branch_pool[l].astype(BF16), w_branch_attn[l].astype(BF16), w_out[l].astype(BF16),
        chunk_cols(up[:, :D_FF]).astype(BF16), chunk_cols(up[:, D_FF:]).astype(BF16),
        chunk_cols(cw[:, :D_FF]), chunk_cols(cw[:, D_FF:]),
        cb[:D_FF].reshape(NCH, 1, FC), cb[D_FF:].reshape(NCH, 1, FC),
        ffn_w_down[l].reshape(NCH, FC, D_MODEL).astype(BF16),
    )
    w_qkv_all = wi[:, POOL_WIDTH:GATE_OFF].astype(BF16)
    return dense, w_qkv_all, row(norm_mix_pre)


def kernel(x_prompt, x_sample, state_pool, cache_k_w128, cache_v_w128, cache_k_w512, cache_v_w512, cache_k_w2048, cache_v_w2048, state_ffn_conv, norm_mix_pre, norm_mix_post, norm_ffn_pre, norm_ffn_post, w_in, b_gate, pool_w, pool_scale, w_branch_pool, w_branch_attn, w_out, ffn_w_up, ffn_conv_w, ffn_conv_b, ffn_w_down):
    depth = w_in.shape[0]
    assert depth == 1, "single-layer trunk"
    assert x_prompt.shape[1:] == (SEQ, D_MODEL) and x_sample.shape[1:] == (DEC_SEQ, D_MODEL)
    l = 0
    bs = x_sample.shape[0]
    W = GROUP_WIDTH
    dense_w, w_qkv_all, g_pre = _layer_weights(
        l, norm_mix_pre, norm_mix_post, norm_ffn_pre, norm_ffn_post, w_in, b_gate, pool_w, pool_scale,
        w_branch_pool, w_branch_attn, w_out, ffn_w_up, ffn_conv_w, ffn_conv_b, ffn_w_down)

    cos_p, sin_p = _rope_tables(jnp.arange(SEQ, dtype=jnp.int32))
    os_p, ls_p, kv_p = [], [], []
    for g, (_, dil) in enumerate(ATTN_GROUPS):
        o, lse, ko, vo = _attn_prompt(x_prompt, g_pre, w_qkv_all[:, 3 * W * g:3 * W * (g + 1)], cos_p, sin_p, dil)
        os_p.append(o)
        ls_p.append(lse)
        kv_p += [ko, vo]
    y_p, pool_p, conv_p = _dense_prompt(x_prompt, os_p, ls_p, dense_w)

    cos_s, sin_s = _rope_tables(PAST_LEN + jnp.arange(DEC_SEQ, dtype=jnp.int32))
    cos_s = jnp.repeat(cos_s, bs, axis=0)
    sin_s = jnp.repeat(sin_s, bs, axis=0)
    x_tm = x_sample.transpose(1, 0, 2).reshape(DEC_SEQ * bs, D_MODEL)
    caches = [c[l].reshape(bs, c.shape[2], W) for c in
              (cache_k_w128, cache_v_w128, cache_k_w512, cache_v_w512, cache_k_w2048, cache_v_w2048)]
    os_s, ls_s, news = _attn_sample(x_tm, g_pre, w_qkv_all, cos_s, sin_s, caches, bs, bb=4)
    caches_t = [c[l].transpose(0, 2, 3, 1).reshape(bs * W, c.shape[2]) for c in
                (cache_k_w128, cache_v_w128, cache_k_w512, cache_v_w512, cache_k_w2048, cache_v_w2048)]
    news_t = [jnp.pad(a.reshape(bs, DEC_SEQ, W).transpose(0, 2, 1).reshape(bs * W, DEC_SEQ),
                      ((0, 0), (LANES - DEC_SEQ, 0))) for a in news]
    rolled = _cache_roll(caches_t, news_t)
    rolled = [r.reshape(bs, HEADS, HEAD_DIM, r.shape[1]).transpose(0, 3, 1, 2) for r in rolled]
    pool_hist_tm = state_pool[l].transpose(1, 0, 2).reshape(POOL_HIST * bs, POOL_WIDTH)
    conv_hist_tm = state_ffn_conv[l].transpose(1, 0, 2).reshape((CONV_W - 1) * bs, 2 * D_FF)
    y_tm, u_tm, conv_tm = _dense_sample(x_tm, os_s, ls_s, pool_hist_tm, conv_hist_tm, dense_w, bs)
    y_s = y_tm.reshape(DEC_SEQ, bs, D_MODEL).transpose(1, 0, 2)
    u_s = u_tm.reshape(DEC_SEQ, bs, POOL_WIDTH).transpose(1, 0, 2)
    pool_s = jnp.concatenate([state_pool[l][:, DEC_SEQ:], u_s], axis=1)
    conv_s = conv_tm.reshape(CONV_W - 1, bs, 2 * D_FF).transpose(1, 0, 2)

    kv_s = [r[None] for r in rolled]
    return (y_p, y_s, pool_p[None], pool_s[None],
            kv_p[0], kv_s[0], kv_p[1], kv_s[1], kv_p[2], kv_s[2], kv_p[3], kv_s[3],
            kv_p[4], kv_s[4], kv_p[5], kv_s[5], conv_p[None], conv_s[None])
```

```python
import functools
import math

import jax
import jax.numpy as jnp
from jax import lax
from jax.experimental import pallas as pl
from jax.experimental.pallas import tpu as pltpu

F32 = jnp.float32
BF16 = jnp.bfloat16

D_MODEL = 1024
SEQ = 2048
DEC_SEQ = 4
PAST_LEN = 8192
POOL_WINDOWS = (2, 4, 8, 16)
POOL_WIDTH = 512
POOL_GROUP = 128
POOL_HIST = 15
ATTN_GROUPS = ((128, 1), (512, 4), (2048, 16))
HEAD_DIM = 64
HEADS = 4
GROUP_WIDTH = 256
ROPE_THETA = 10000.0
QB = 128
D_FF = 2816
CONV_W = 3
RMS_EPS = 1e-6
GATE_OFF = POOL_WIDTH + 3 * 3 * GROUP_WIDTH

LANES = 128
SUBLANES = 8
TS = 512
FC = 256
NCH = D_FF // FC
NEG = -1e30
QK_SCALE = 1.0 / math.sqrt(HEAD_DIM)
VMEM_LIMIT_V7X = 60000 * 1024


def _rms(x, g):
    ms = jnp.mean(x * x, axis=-1, keepdims=True)
    return x * lax.rsqrt(ms + RMS_EPS) * g


def _rope(x, cos, sin_signed):
    lane = lax.broadcasted_iota(jnp.int32, x.shape, 1)
    first_half = (lane % HEAD_DIM) < (HEAD_DIM // 2)
    swapped = jnp.where(first_half,
                        pltpu.roll(x, GROUP_WIDTH - HEAD_DIM // 2, 1),
                        pltpu.roll(x, HEAD_DIM // 2, 1))
    return x * cos + swapped * sin_signed


def _attn_prompt_kernel(x_ref, g_ref, w_ref, cos_ref, sin_ref,
                        attn_ref, k0_ref, v0_ref, k1_ref, v1_ref, k2_ref, v2_ref,
                        h_s, qkv_s, run_s):
    n_chunks = SEQ // TS
    kv_refs = ((k0_ref, v0_ref), (k1_ref, v1_ref), (k2_ref, v2_ref))
    W = GROUP_WIDTH
    for c in range(n_chunks):
        rows = slice(c * TS, (c + 1) * TS)
        h_s[rows] = _rms(x_ref[rows], g_ref[...]).astype(BF16)

    for g, (window, dil) in enumerate(ATTN_GROUPS):
        keep = min(window, SEQ)
        for c in range(n_chunks):
            rows = slice(c * TS, (c + 1) * TS)
            qkv = jnp.dot(h_s[rows], w_ref[:, 3 * W * g:3 * W * (g + 1)], preferred_element_type=F32)
            cos = cos_ref[rows]
            sin = sin_ref[rows]
            q = _rope(qkv[:, :W], cos, sin) * QK_SCALE
            k = _rope(qkv[:, W:2 * W], cos, sin)
            v = qkv[:, 2 * W:]
            for s, a in enumerate((q, k, v)):
                qkv_s[2 * s, rows] = a[:, :LANES]
                qkv_s[2 * s + 1, rows] = a[:, LANES:]
            lo = max(SEQ - keep, c * TS)
            if lo < (c + 1) * TS:
                r0 = lo - c * TS
                kv_refs[g][0][:, lo - (SEQ - keep):(c + 1) * TS - (SEQ - keep)] = k[r0:].T
                kv_refs[g][1][:, lo - (SEQ - keep):(c + 1) * TS - (SEQ - keep)] = v[r0:].T

        m_len = SEQ // dil
        nb = m_len // QB
        nk = 2 * QB if nb > 1 else QB

        def rows_of(start, n, dil=dil):
            return pl.ds(start, n, stride=dil) if dil > 1 else pl.ds(start, n)

        def block(n, carry, g=g, dil=dil, nb=nb, nk=nk, rows_of=rows_of):
            if nb > 1:
                r, jb = n // nb, n % nb
                off = jnp.where(jb > 0, QB, 0)
            else:
                r, jb, off = n, 0, 0
            start = r + dil * QB * jb
            wstart = start - dil * off
            lane_head = lax.broadcasted_iota(jnp.int32, (QB, W), 1) // HEAD_DIM
            lane_even = lax.broadcasted_iota(jnp.int32, (QB, LANES), 1) < HEAD_DIM
            diff = (lax.broadcasted_iota(jnp.int32, (QB, nk), 0) + off
                    - lax.broadcasted_iota(jnp.int32, (QB, nk), 1))
            bias = jnp.where((diff >= 0) & (diff <= QB), 0.0, NEG)
            qb = jnp.concatenate([qkv_s[0, rows_of(start, QB), :], qkv_s[1, rows_of(start, QB), :]], axis=1)
            kw = jnp.concatenate([qkv_s[2, rows_of(wstart, nk), :], qkv_s[3, rows_of(wstart, nk), :]],
                                 axis=1).astype(BF16)
            qm = jnp.concatenate([jnp.where(lane_head == hh, qb, 0.0) for hh in range(HEADS)],
                                 axis=0).astype(BF16)
            s = lax.dot_general(qm, kw, (((1,), (1,)), ((), ())), preferred_element_type=F32)
            s = s + jnp.concatenate([bias] * HEADS, axis=0)
            m = jnp.max(s, axis=-1, keepdims=True)
            p = jnp.exp(s - m)
            l = jnp.sum(p, axis=-1, keepdims=True)
            p16 = p.astype(BF16)
            for pr in range(2):
                vs = qkv_s[4 + pr, rows_of(wstart, nk), :].astype(BF16)
                h0 = slice(2 * pr * QB, (2 * pr + 1) * QB)
                h1 = slice((2 * pr + 1) * QB, (2 * pr + 2) * QB)
                acc = jnp.where(lane_even, jnp.dot(p16[h0], vs, preferred_element_type=F32),
                                jnp.dot(p16[h1], vs, preferred_element_type=F32))
                mm = jnp.where(lane_even, m[h0], m[h1])
                ll = jnp.where(lane_even, l[h0], l[h1])
                dst = rows_of(start, QB)
                if g == 0:
                    run_s[pr, dst, :] = mm
                    run_s[2 + pr, dst, :] = ll
                    run_s[4 + pr, dst, :] = acc
                else:
                    mo = run_s[pr, dst, :]
                    mn = jnp.maximum(mo, mm)
                    a = jnp.exp(mo - mn)
                    b = jnp.exp(mm - mn)
                    run_s[pr, dst, :] = mn
                    run_s[2 + pr, dst, :] = a * run_s[2 + pr, dst, :] + b * ll
                    run_s[4 + pr, dst, :] = a * run_s[4 + pr, dst, :] + b * acc
            return carry

        lax.fori_loop(0, SEQ // QB, block, 0, unroll=2)

    for c in range(n_chunks):
        rows = slice(c * TS, (c + 1) * TS)
        attn_ref[rows, :] = jnp.concatenate([run_s[4, rows] / run_s[2, rows], run_s[5, rows] / run_s[3, rows]],
                                            axis=1).astype(BF16)


def _attn_prompt(x, g_pre, w_qkv_all, cos_t, sin_t):
    B = x.shape[0]
    W = GROUP_WIDTH
    keeps = [min(w, SEQ) for w, _ in ATTN_GROUPS]
    one = pl.Buffered(1)
    const = lambda shape: pl.BlockSpec(shape, lambda b: (0,) * len(shape), pipeline_mode=one)
    kv_specs, kv_shapes = [], []
    for keep in keeps:
        kv_specs += [pl.BlockSpec((None, W, keep), lambda b: (b, 0, 0))] * 2
        kv_shapes += [jax.ShapeDtypeStruct((B, W, keep), F32)] * 2
    outs = pl.pallas_call(
        _attn_prompt_kernel,
        grid=(B,),
        in_specs=[pl.BlockSpec((None, SEQ, D_MODEL), lambda b: (b, 0, 0), pipeline_mode=one),
                  const((1, D_MODEL)), const((D_MODEL, 9 * W)), const((SEQ, W)), const((SEQ, W))],
        out_specs=[pl.BlockSpec((None, SEQ, W), lambda b: (b, 0, 0))] + kv_specs,
        out_shape=[jax.ShapeDtypeStruct((B, SEQ, W), BF16)] + kv_shapes,
        scratch_shapes=[pltpu.VMEM((SEQ, D_MODEL), BF16),
                        pltpu.VMEM((6, SEQ, LANES), F32),
                        pltpu.VMEM((6, SEQ, LANES), F32)],
        compiler_params=pltpu.CompilerParams(dimension_semantics=("arbitrary",),
                                             vmem_limit_bytes=VMEM_LIMIT_V7X),
        name="attn_prompt",
    )(x, g_pre, w_qkv_all, cos_t, sin_t)
    kv = [a.reshape(B, HEADS, HEAD_DIM, a.shape[2]).transpose(0, 3, 1, 2)[None] for a in outs[1:]]
    return outs[0], kv


def _gelu_tanh(x):
    c = math.sqrt(2.0 / math.pi)
    u = x * (2.0 * c + (2.0 * c * 0.044715) * (x * x))
    return x / (1.0 + jnp.exp(-u))


def _dense_body(x, attn, wr, upbuf, *, roll_steps, u_hist, cnt_of, keep_u, up_hist, keep_up):
    (gpre, gpost, gfpre, gfpost, wpool, wgate, bgate, poolw, pscale, wbp, wba, wout,
     wupa, wupb, cwa, cwb, cba, cbb, wdown) = wr
    R = x.shape[0]
    h = _rms(x, gpre[...]).astype(BF16)

    u = jnp.dot(h, wpool[...], preferred_element_type=F32)
    keep_u(u)
    hist = u_hist()
    H = hist.shape[0]
    s = jnp.concatenate([hist, u], axis=0)
    ys = []
    for gi, w in enumerate(POOL_WINDOWS):
        c0 = gi * POOL_GROUP
        s = s[:, POOL_GROUP:] if gi else s
        s = s + roll_steps(s, w // 2)
        zmix = s[H:, :POOL_GROUP] / cnt_of(w) - u[:, c0:c0 + POOL_GROUP]
        ys.append(jnp.dot(zmix.astype(BF16), poolw[gi], preferred_element_type=F32))
    pool_out = jnp.concatenate(ys, axis=1) * pscale[...]
    a_br = jnp.dot(pool_out.astype(BF16), wbp[...], preferred_element_type=F32)

    b_br = jnp.dot(attn.astype(BF16), wba[...], preferred_element_type=F32)

    gates = jax.nn.sigmoid(jnp.dot(h, wgate[...], preferred_element_type=F32) + bgate[...])
    merged = gates[:, :D_MODEL] * a_br + gates[:, D_MODEL:] * b_br
    mix = jnp.dot(merged.astype(BF16), wout[...], preferred_element_type=F32)
    x1 = x + _rms(mix, gpost[...])

    hn = _rms(x1, gfpre[...]).astype(BF16)
    acc = jnp.zeros((R, D_MODEL), F32)

    def project(c):
        upbuf[c % 2, 0] = jnp.dot(hn, wupa[c], preferred_element_type=F32)
        upbuf[c % 2, 1] = jnp.dot(hn, wupb[c], preferred_element_type=F32)

    project(0)
    for c in range(NCH):
        if c + 1 < NCH:
            project(c + 1)
        ups = (upbuf[c % 2, 0], upbuf[c % 2, 1])
        keep_up(c, *ups)
        conv = []
        for half, (up, cw, cb) in enumerate(zip(ups, (cwa, cwb), (cba, cbb))):
            w = cw[c]
            hist_c = up_hist(c, half)
            hc = hist_c.shape[0]
            ext = jnp.concatenate([hist_c, up], axis=0)
            y = cb[c] + up * w[CONV_W - 1:CONV_W]
            for j in range(1, CONV_W):
                ext = roll_steps(ext, 1)
                y = y + ext[hc:] * w[CONV_W - 1 - j:CONV_W - j]
            conv.append(y)
        act = _gelu_tanh(conv[0]) * conv[1]
        acc = acc + jnp.dot(act.astype(BF16), wdown[c], preferred_element_type=F32)
    return x1 + _rms(acc, gfpost[...])


N_W = 19
HU_ROWS = 16
HC_ROWS = 8


def _dense_prompt_kernel(*refs):
    x_ref, attn_ref = refs[0:2]
    wr = refs[2:2 + N_W]
    y_ref, pool_ref, conv_ref = refs[2 + N_W:5 + N_W]
    uhist, chist, upbuf = refs[5 + N_W:]
    i = pl.program_id(1)

    @pl.when(i == 0)
    def _():
        uhist[...] = jnp.zeros(uhist.shape, F32)
        chist[...] = jnp.zeros(chist.shape, F32)

    pos = i * TS + lax.broadcasted_iota(jnp.int32, (TS, 1), 0)
    hist_u = uhist[...]

    def keep_u(u):
        uhist[...] = u[TS - HU_ROWS:]
        pool_ref[...] = u[TS - POOL_HIST:]

    def keep_up(c, ua, ub):
        for half, up in enumerate((ua, ub)):
            conv_ref[:, half * D_FF + c * FC:half * D_FF + (c + 1) * FC] = up[TS - (CONV_W - 1):]

    def up_hist(c, half):
        return chist[2 * c + half]

    def roll_steps(a, j):
        return pltpu.roll(a, j, 0)

    x = x_ref[...]
    wr_hist = []

    def keep_up_and_hist(c, ua, ub):
        keep_up(c, ua, ub)
        wr_hist.append((c, ua[TS - HC_ROWS:], ub[TS - HC_ROWS:]))

    y_ref[...] = _dense_body(x, attn_ref[...], wr, upbuf, roll_steps=roll_steps, u_hist=lambda: hist_u,
                             cnt_of=lambda w: jnp.minimum(pos + 1, w).astype(F32),
                             keep_u=keep_u, up_hist=up_hist, keep_up=keep_up_and_hist)
    for c, ta, tb in wr_hist:
        chist[2 * c] = ta
        chist[2 * c + 1] = tb


def _dense_sample_kernel(*refs, bs):
    x_ref, attn_ref, ph_ref, ch_ref = refs[0:4]
    wr = refs[4:4 + N_W]
    y_ref, u_ref, conv_ref = refs[4 + N_W:7 + N_W]
    upbuf = refs[7 + N_W]
    R = DEC_SEQ * bs
    HC = (CONV_W - 1) * bs

    def keep_u(u):
        u_ref[...] = u

    def keep_up(c, ua, ub):
        conv_ref[:, c * FC:(c + 1) * FC] = ua[R - HC:]
        conv_ref[:, D_FF + c * FC:D_FF + (c + 1) * FC] = ub[R - HC:]

    def up_hist(c, half):
        c0 = half * D_FF + c * FC
        return ch_ref[:, c0:c0 + FC]

    def roll_steps(a, j):
        return jnp.concatenate([a[a.shape[0] - j * bs:], a[:a.shape[0] - j * bs]], axis=0)

    y_ref[...] = _dense_body(x_ref[...], attn_ref[...], wr, upbuf, roll_steps=roll_steps, u_hist=lambda: ph_ref[...],
                             cnt_of=float, keep_u=keep_u, up_hist=up_hist, keep_up=keep_up)


def _weight_shapes():
    return [
        (1, D_MODEL), (1, D_MODEL), (1, D_MODEL), (1, D_MODEL),
        (D_MODEL, POOL_WIDTH), (D_MODEL, 2 * D_MODEL), (1, 2 * D_MODEL),
        (len(POOL_WINDOWS), POOL_GROUP, POOL_GROUP), (1, POOL_WIDTH),
        (POOL_WIDTH, D_MODEL), (GROUP_WIDTH, D_MODEL), (D_MODEL, D_MODEL),
        (NCH, D_MODEL, FC), (NCH, D_MODEL, FC),
        (NCH, CONV_W, FC), (NCH, CONV_W, FC),
        (NCH, 1, FC), (NCH, 1, FC),
        (NCH, FC, D_MODEL),
    ]


def _dense_prompt(x, attn, weights):
    B = x.shape[0]
    steps = SEQ // TS
    tile = lambda w: pl.BlockSpec((None, TS, w), lambda b, i: (b, i, 0))
    wspecs = [pl.BlockSpec(s, lambda b, i, n=len(s): (0,) * n, pipeline_mode=pl.Buffered(1)) for s in _weight_shapes()]
    y, pool_p, conv_p = pl.pallas_call(
        _dense_prompt_kernel,
        grid=(B, steps),
        in_specs=[tile(D_MODEL), tile(GROUP_WIDTH)] + wspecs,
        out_specs=[tile(D_MODEL),
                   pl.BlockSpec((None, POOL_HIST, POOL_WIDTH), lambda b, i: (b, 0, 0)),
                   pl.BlockSpec((None, CONV_W - 1, 2 * D_FF), lambda b, i: (b, 0, 0))],
        out_shape=[jax.ShapeDtypeStruct((B, SEQ, D_MODEL), F32),
                   jax.ShapeDtypeStruct((B, POOL_HIST, POOL_WIDTH), F32),
                   jax.ShapeDtypeStruct((B, CONV_W - 1, 2 * D_FF), F32)],
        scratch_shapes=[pltpu.VMEM((HU_ROWS, POOL_WIDTH), F32),
                        pltpu.VMEM((2 * NCH, HC_ROWS, FC), F32),
                        pltpu.VMEM((2, 2, TS, FC), F32)],
        compiler_params=pltpu.CompilerParams(dimension_semantics=("arbitrary", "arbitrary"),
                                             vmem_limit_bytes=VMEM_LIMIT_V7X),
        name="dense_prompt",
    )(x, attn, *weights)
    return y, pool_p, conv_p


def _dense_sample(x_tm, attn_tm, pool_hist_tm, conv_hist_tm, weights, bs):
    R = DEC_SEQ * bs
    full = lambda shape: pl.BlockSpec(shape, lambda n=len(shape): (0,) * n)
    y, u, conv_s = pl.pallas_call(
        functools.partial(_dense_sample_kernel, bs=bs),
        in_specs=[full((R, D_MODEL)), full((R, GROUP_WIDTH)),
                  full((POOL_HIST * bs, POOL_WIDTH)), full(((CONV_W - 1) * bs, 2 * D_FF))]
                 + [full(s) for s in _weight_shapes()],
        out_specs=[full((R, D_MODEL)), full((R, POOL_WIDTH)), full(((CONV_W - 1) * bs, 2 * D_FF))],
        out_shape=[jax.ShapeDtypeStruct((R, D_MODEL), F32),
                   jax.ShapeDtypeStruct((R, POOL_WIDTH), F32),
                   jax.ShapeDtypeStruct(((CONV_W - 1) * bs, 2 * D_FF), F32)],
        scratch_shapes=[pltpu.VMEM((2, 2, R, FC), F32)],
        compiler_params=pltpu.CompilerParams(vmem_limit_bytes=VMEM_LIMIT_V7X),
        name="dense_sample",
    )(x_tm, attn_tm, pool_hist_tm, conv_hist_tm, *weights)
    return y, u, conv_s


def _attn_sample_kernel(x_ref, g_ref, w_ref, cos_ref, sin_ref,
                        k0_ref, v0_ref, k1_ref, v1_ref, k2_ref, v2_ref,
                        attn_ref, kn0_ref, vn0_ref, kn1_ref, vn1_ref, kn2_ref, vn2_ref,
                        qkv_s, sn_s, *, bs, bb):
    bi = pl.program_id(0)
    T = DEC_SEQ
    W = GROUP_WIDTH
    row_h = lax.broadcasted_iota(jnp.int32, (W, W), 0) // HEAD_DIM
    col_h = lax.broadcasted_iota(jnp.int32, (W, W), 1) // HEAD_DIM
    seg = jnp.where(row_h == col_h, 1.0, 0.0).astype(BF16)

    def head_sum(prod):
        return jnp.dot(prod.astype(BF16), seg, preferred_element_type=F32)

    slots = {}
    for g in range(3):
        for t in range(T):
            for tp in (range(t + 1) if g == 0 else (t,)):
                slots[(g, t, tp)] = len(slots)

    @pl.when(bi == 0)
    def _():
        h = _rms(x_ref[...], g_ref[...]).astype(BF16)
        qkv = jnp.dot(h, w_ref[...], preferred_element_type=F32)
        cos = cos_ref[...]
        sin = sin_ref[...]
        for g in range(3):
            b0 = 3 * W * g
            q = _rope(qkv[:, b0:b0 + W], cos, sin) * QK_SCALE
            k = _rope(qkv[:, b0 + W:b0 + 2 * W], cos, sin)
            qkv_s[:, b0:b0 + W] = q
            qkv_s[:, b0 + W:b0 + 2 * W] = k
            qkv_s[:, b0 + 2 * W:b0 + 3 * W] = qkv[:, b0 + 2 * W:b0 + 3 * W]
            for t in range(T):
                for tp in (range(t + 1) if g == 0 else (t,)):
                    sn_s[slots[(g, t, tp)]] = head_sum(q[t * bs:(t + 1) * bs] * k[tp * bs:(tp + 1) * bs])

    krefs = (k0_ref, k1_ref, k2_ref)
    vrefs = (v0_ref, v1_ref, v2_ref)
    knrefs = (kn0_ref, kn1_ref, kn2_ref)
    vnrefs = (vn0_ref, vn1_ref, vn2_ref)
    key_row = lax.broadcasted_iota(jnp.int32, (QB, W), 0)

    def per_b(j, carry_):
        b = bi * bb + j
        for t in range(T):
            row = t * bs + b
            stats = []
            for g in range(3):
                b0 = 3 * W * g
                qrow = qkv_s[pl.ds(row, 1), b0:b0 + W]
                knrefs[g][pl.ds(b * T + t, 1), :] = qkv_s[pl.ds(row, 1), b0 + W:b0 + 2 * W]
                vnrefs[g][pl.ds(b * T + t, 1), :] = qkv_s[pl.ds(row, 1), b0 + 2 * W:b0 + 3 * W]
                if g == 0:
                    kc = krefs[g][j]
                    vc = vrefs[g][j]
                else:
                    kc = krefs[g][j, :, t * W:(t + 1) * W]
                    vc = vrefs[g][j, :, t * W:(t + 1) * W]
                sc = head_sum(kc * qrow)
                if g == 0:
                    sc = jnp.where(key_row >= t, sc, NEG)
                tps = tuple(range(t + 1)) if g == 0 else (t,)
                sn = [sn_s[slots[(g, t, tp)], pl.ds(b, 1), :] for tp in tps]
                m = jnp.max(sc, axis=0, keepdims=True)
                for s1 in sn:
                    m = jnp.maximum(m, s1)
                p = jnp.exp(sc - m)
                l = jnp.sum(p, axis=0, keepdims=True)
                acc = jnp.sum(p * vc, axis=0, keepdims=True)
                for tp, s1 in zip(tps, sn):
                    pn = jnp.exp(s1 - m)
                    l = l + pn
                    acc = acc + pn * qkv_s[pl.ds(tp * bs + b, 1), b0 + 2 * W:b0 + 3 * W]
                stats.append((m, l, acc))
            mx = jnp.maximum(jnp.maximum(stats[0][0], stats[1][0]), stats[2][0])
            den = 0.0
            num = 0.0
            for m, l, acc in stats:
                e = jnp.exp(m - mx)
                den = den + e * l
                num = num + e * acc
            attn_ref[pl.ds(row, 1), :] = num / den
        return carry_

    lax.fori_loop(0, bb, per_b, 0)


def _attn_sample(x_tm, g_pre, w_qkv_all, cos_s, sin_s, caches, bs, bb):
    R = DEC_SEQ * bs
    W = GROUP_WIDTH
    views, cspecs = [], []
    for idx, c in enumerate(caches):
        dil = ATTN_GROUPS[idx // 2][1]
        wb = c.shape[1]
        views.append(c.reshape(bs, wb // dil, dil * W))
        width = W if dil == 1 else DEC_SEQ * W
        cspecs.append(pl.BlockSpec((bb, QB, width), lambda i: (i, 0, 0)))
    full = lambda shape: pl.BlockSpec(shape, lambda i: (0,) * len(shape))
    n_slots = sum(t + 1 for t in range(DEC_SEQ)) + 2 * DEC_SEQ
    outs = pl.pallas_call(
        functools.partial(_attn_sample_kernel, bs=bs, bb=bb),
        grid=(bs // bb,),
        in_specs=[full((R, D_MODEL)), full((1, D_MODEL)), full((D_MODEL, 9 * W)), full((R, W)), full((R, W))] + cspecs,
        out_specs=[full((R, W))] * 7,
        out_shape=[jax.ShapeDtypeStruct((R, W), F32)] * 7,
        scratch_shapes=[pltpu.VMEM((R, 9 * W), F32), pltpu.VMEM((n_slots, bs, W), F32)],
        compiler_params=pltpu.CompilerParams(dimension_semantics=("arbitrary",),
                                             vmem_limit_bytes=VMEM_LIMIT_V7X),
        name="attn_sample",
    )(x_tm, g_pre, w_qkv_all, cos_s, sin_s, *views)
    return outs[0], outs[1:7]


ROLL_ROWS = 512


def _cache_roll_kernel(*refs):
    n = len(refs) // 3
    caches, news, outs = refs[:n], refs[n:2 * n], refs[2 * n:]
    lane = lax.broadcasted_iota(jnp.int32, (ROLL_ROWS, LANES), 1)
    for c_ref, n_ref, o_ref in zip(caches, news, outs):
        wb = c_ref.shape[-1]
        rolled = pltpu.roll(c_ref[...], wb - DEC_SEQ, 1)
        if wb > LANES:
            o_ref[:, :wb - LANES] = rolled[:, :wb - LANES]
        o_ref[:, wb - LANES:] = jnp.where(lane >= LANES - DEC_SEQ, n_ref[...], rolled[:, wb - LANES:])


def _cache_roll(caches_t, news_t):
    rows = caches_t[0].shape[0]
    spec = lambda wb: pl.BlockSpec((ROLL_ROWS, wb), lambda i: (i, 0))
    cspecs = [spec(c.shape[1]) for c in caches_t]
    return pl.pallas_call(
        _cache_roll_kernel,
        grid=(rows // ROLL_ROWS,),
        in_specs=cspecs + [spec(LANES)] * len(caches_t),
        out_specs=cspecs,
        out_shape=[jax.ShapeDtypeStruct(c.shape, c.dtype) for c in caches_t],
        compiler_params=pltpu.CompilerParams(dimension_semantics=("arbitrary",),
                                             vmem_limit_bytes=VMEM_LIMIT_V7X),
        name="cache_roll",
    )(*caches_t, *news_t)


def _rope_tables(pos):
    inv_freq = ROPE_THETA ** (-jnp.arange(0, HEAD_DIM, 2, dtype=F32) / HEAD_DIM)
    ang = pos.astype(F32)[:, None] * inv_freq[None, :]
    cos = jnp.cos(ang)
    sin = jnp.sin(ang)
    cos_t = jnp.tile(jnp.concatenate([cos, cos], axis=-1), (1, HEADS))
    sin_t = jnp.tile(jnp.concatenate([-sin, sin], axis=-1), (1, HEADS))
    return cos_t, sin_t


def _layer_weights(l, norm_mix_pre, norm_mix_post, norm_ffn_pre, norm_ffn_post, w_in, b_gate, pool_w, pool_scale,
                   w_branch_pool, w_branch_attn, w_out, ffn_w_up, ffn_conv_w, ffn_conv_b, ffn_w_down):
    row = lambda a: a[l].reshape(1, -1)
    chunk_cols = lambda a: a.reshape(a.shape[0], NCH, FC).transpose(1, 0, 2)
    wi = w_in[l]
    up = ffn_w_up[l]
    cw = ffn_conv_w[l]
    cb = ffn_conv_b[l]
    dense = (
        row(norm_mix_pre), row(norm_mix_post), row(norm_ffn_pre), row(norm_ffn_post),
        wi[:, :POOL_WIDTH].astype(BF16), wi[:, GATE_OFF:].astype(BF16), row(b_gate),
        pool_w[l].astype(BF16), row(pool_scale),
        w_branch_pool[l].astype(BF16), w_branch_attn[l].astype(BF16), w_out[l].astype(BF16),
        chunk_cols(up[:, :D_FF]).astype(BF16), chunk_cols(up[:, D_FF:]).astype(BF16),
        chunk_cols(cw[:, :D_FF]), chunk_cols(cw[:, D_FF:]),
        cb[:D_FF].reshape(NCH, 1, FC), cb[D_FF:].reshape(NCH, 1, FC),
        ffn_w_down[l].reshape(NCH, FC, D_MODEL).astype(BF16),
    )
    w_qkv_all = wi[:, POOL_WIDTH:GATE_OFF].astype(BF16)
    return dense, w_qkv_all, row(norm_mix_pre)


def kernel(x_prompt, x_sample, state_pool, cache_k_w128, cache_v_w128, cache_k_w512, cache_v_w512, cache_k_w2048, cache_v_w2048, state_ffn_conv, norm_mix_pre, norm_mix_post, norm_ffn_pre, norm_ffn_post, w_in, b_gate, pool_w, pool_scale, w_branch_pool, w_branch_attn, w_out, ffn_w_up, ffn_conv_w, ffn_conv_b, ffn_w_down):
    depth = w_in.shape[0]
    assert depth == 1, "single-layer trunk"
    assert x_prompt.shape[1:] == (SEQ, D_MODEL) and x_sample.shape[1:] == (DEC_SEQ, D_MODEL)
    l = 0
    bs = x_sample.shape[0]
    W = GROUP_WIDTH
    dense_w, w_qkv_all, g_pre = _layer_weights(
        l, norm_mix_pre, norm_mix_post, norm_ffn_pre, norm_ffn_post, w_in, b_gate, pool_w, pool_scale,
        w_branch_pool, w_branch_attn, w_out, ffn_w_up, ffn_conv_w, ffn_conv_b, ffn_w_down)

    cos_p, sin_p = _rope_tables(jnp.arange(SEQ, dtype=jnp.int32))
    attn_p, kv_p = _attn_prompt(x_prompt, g_pre, w_qkv_all, cos_p, sin_p)
    y_p, pool_p, conv_p = _dense_prompt(x_prompt, attn_p, dense_w)

    cos_s, sin_s = _rope_tables(PAST_LEN + jnp.arange(DEC_SEQ, dtype=jnp.int32))
    cos_s = jnp.repeat(cos_s, bs, axis=0)
    sin_s = jnp.repeat(sin_s, bs, axis=0)
    x_tm = x_sample.transpose(1, 0, 2).reshape(DEC_SEQ * bs, D_MODEL)
    cache_in = (cache_k_w128, cache_v_w128, cache_k_w512, cache_v_w512, cache_k_w2048, cache_v_w2048)
    caches = [c[l].reshape(bs, c.shape[2], W) for c in cache_in]
    attn_s, news = _attn_sample(x_tm, g_pre, w_qkv_all, cos_s, sin_s, caches, bs, bb=4)
    caches_t = [c[l].transpose(0, 2, 3, 1).reshape(bs * W, c.shape[2]) for c in cache_in]
    news_t = [jnp.pad(a.reshape(bs, DEC_SEQ, W).transpose(0, 2, 1).reshape(bs * W, DEC_SEQ),
                      ((0, 0), (LANES - DEC_SEQ, 0))) for a in news]
    rolled = _cache_roll(caches_t, news_t)
    kv_s = [r.reshape(bs, HEADS, HEAD_DIM, r.shape[1]).transpose(0, 3, 1, 2)[None] for r in rolled]
    pool_hist_tm = state_pool[l].transpose(1, 0, 2).reshape(POOL_HIST * bs, POOL_WIDTH)
    conv_hist_tm = state_ffn_conv[l].transpose(1, 0, 2).reshape((CONV_W - 1) * bs, 2 * D_FF)
    y_tm, u_tm, conv_tm = _dense_sample(x_tm, attn_s, pool_hist_tm, conv_hist_tm, dense_w, bs)
    y_s = y_tm.reshape(DEC_SEQ, bs, D_MODEL).transpose(1, 0, 2)
    u_s = u_tm.reshape(DEC_SEQ, bs, POOL_WIDTH).transpose(1, 0, 2)
    pool_s = jnp.concatenate([state_pool[l][:, DEC_SEQ:], u_s], axis=1)
    conv_s = conv_tm.reshape(CONV_W - 1, bs, 2 * D_FF).transpose(1, 0, 2)

    return (y_p, y_s, pool_p[None], pool_s[None],
            kv_p[0], kv_s[0], kv_p[1], kv_s[1], kv_p[2], kv_s[2], kv_p[3], kv_s[3],
            kv_p[4], kv_s[4], kv_p[5], kv_s[5], conv_p[None], conv_s[None])
```

```python
import functools
import math

import jax
import jax.numpy as jnp
from jax import lax
from jax.experimental import pallas as pl
from jax.experimental.pallas import tpu as pltpu

F32 = jnp.float32
BF16 = jnp.bfloat16

D_MODEL = 1024
SEQ = 2048
DEC_SEQ = 4
PAST_LEN = 8192
POOL_WINDOWS = (2, 4, 8, 16)
POOL_WIDTH = 512
POOL_GROUP = 128
POOL_HIST = 15
ATTN_GROUPS = ((128, 1), (512, 4), (2048, 16))
HEAD_DIM = 64
HEADS = 4
GROUP_WIDTH = 256
ROPE_THETA = 10000.0
QB = 128
D_FF = 2816
CONV_W = 3
RMS_EPS = 1e-6
GATE_OFF = POOL_WIDTH + 3 * 3 * GROUP_WIDTH

LANES = 128
SUBLANES = 8
TS = 512
FC = 256
NCH = D_FF // FC
NEG = -1e30
QK_SCALE = 1.0 / math.sqrt(HEAD_DIM)
VMEM_LIMIT_V7X = 60000 * 1024


def _rms(x, g):
    ms = jnp.mean(x * x, axis=-1, keepdims=True)
    return x * lax.rsqrt(ms + RMS_EPS) * g


def _rope(x, cos, sin_signed):
    lane = lax.broadcasted_iota(jnp.int32, x.shape, 1)
    first_half = (lane % HEAD_DIM) < (HEAD_DIM // 2)
    swapped = jnp.where(first_half,
                        pltpu.roll(x, GROUP_WIDTH - HEAD_DIM // 2, 1),
                        pltpu.roll(x, HEAD_DIM // 2, 1))
    return x * cos + swapped * sin_signed


def _attn_prompt_kernel(x_ref, g_ref, w_ref, cos_ref, sin_ref,
                        attn_ref, k0_ref, v0_ref, k1_ref, v1_ref, k2_ref, v2_ref,
                        h_s, qkv_s, run_s):
    n_chunks = SEQ // TS
    kv_refs = ((k0_ref, v0_ref), (k1_ref, v1_ref), (k2_ref, v2_ref))
    W = GROUP_WIDTH
    for c in range(n_chunks):
        rows = slice(c * TS, (c + 1) * TS)
        h_s[rows] = _rms(x_ref[rows], g_ref[...]).astype(BF16)

    for g, (window, dil) in enumerate(ATTN_GROUPS):
        keep = min(window, SEQ)
        for c in range(n_chunks):
            rows = slice(c * TS, (c + 1) * TS)
            qkv = jnp.dot(h_s[rows], w_ref[:, 3 * W * g:3 * W * (g + 1)], preferred_element_type=F32)
            cos = cos_ref[rows]
            sin = sin_ref[rows]
            q = _rope(qkv[:, :W], cos, sin) * QK_SCALE
            k = _rope(qkv[:, W:2 * W], cos, sin)
            v = qkv[:, 2 * W:]
            for s, a in enumerate((q, k, v)):
                qkv_s[2 * s, rows] = a[:, :LANES]
                qkv_s[2 * s + 1, rows] = a[:, LANES:]
            lo = max(SEQ - keep, c * TS)
            if lo < (c + 1) * TS:
                r0 = lo - c * TS
                kv_refs[g][0][:, lo - (SEQ - keep):(c + 1) * TS - (SEQ - keep)] = k[r0:].T
                kv_refs[g][1][:, lo - (SEQ - keep):(c + 1) * TS - (SEQ - keep)] = v[r0:].T

        m_len = SEQ // dil
        nb = m_len // QB
        nk = 2 * QB if nb > 1 else QB

        def rows_of(start, n, dil=dil):
            return pl.ds(start, n, stride=dil) if dil > 1 else pl.ds(start, n)

        def block(n, carry, g=g, dil=dil, nb=nb, nk=nk, rows_of=rows_of):
            if nb > 1:
                r, jb = n // nb, n % nb
                off = jnp.where(jb > 0, QB, 0)
            else:
                r, jb, off = n, 0, 0
            start = r + dil * QB * jb
            wstart = start - dil * off
            lane_head = lax.broadcasted_iota(jnp.int32, (QB, W), 1) // HEAD_DIM
            lane_even = lax.broadcasted_iota(jnp.int32, (QB, LANES), 1) < HEAD_DIM
            diff = (lax.broadcasted_iota(jnp.int32, (QB, nk), 0) + off
                    - lax.broadcasted_iota(jnp.int32, (QB, nk), 1))
            bias = jnp.where((diff >= 0) & (diff <= QB), 0.0, NEG)
            qb = jnp.concatenate([qkv_s[0, rows_of(start, QB), :], qkv_s[1, rows_of(start, QB), :]], axis=1)
            kw = jnp.concatenate([qkv_s[2, rows_of(wstart, nk), :], qkv_s[3, rows_of(wstart, nk), :]],
                                 axis=1).astype(BF16)
            qm = jnp.concatenate([jnp.where(lane_head == hh, qb, 0.0) for hh in range(HEADS)],
                                 axis=0).astype(BF16)
            s = lax.dot_general(qm, kw, (((1,), (1,)), ((), ())), preferred_element_type=F32)
            s = s + jnp.concatenate([bias] * HEADS, axis=0)
            m = jnp.max(s, axis=-1, keepdims=True)
            p = jnp.exp(s - m)
            l = jnp.sum(p, axis=-1, keepdims=True)
            p16 = p.astype(BF16)
            for pr in range(2):
                vs = qkv_s[4 + pr, rows_of(wstart, nk), :].astype(BF16)
                h0 = slice(2 * pr * QB, (2 * pr + 1) * QB)
                h1 = slice((2 * pr + 1) * QB, (2 * pr + 2) * QB)
                acc = jnp.where(lane_even, jnp.dot(p16[h0], vs, preferred_element_type=F32),
                                jnp.dot(p16[h1], vs, preferred_element_type=F32))
                mm = jnp.where(lane_even, m[h0], m[h1])
                ll = jnp.where(lane_even, l[h0], l[h1])
                dst = rows_of(start, QB)
                if g == 0:
                    run_s[pr, dst, :] = mm
                    run_s[2 + pr, dst, :] = ll
                    run_s[4 + pr, dst, :] = acc
                else:
                    mo = run_s[pr, dst, :]
                    mn = jnp.maximum(mo, mm)
                    a = jnp.exp(mo - mn)
                    b = jnp.exp(mm - mn)
                    run_s[pr, dst, :] = mn
                    run_s[2 + pr, dst, :] = a * run_s[2 + pr, dst, :] + b * ll
                    run_s[4 + pr, dst, :] = a * run_s[4 + pr, dst, :] + b * acc
            return carry

        lax.fori_loop(0, SEQ // QB, block, 0, unroll=4)

    for c in range(n_chunks):
        rows = slice(c * TS, (c + 1) * TS)
        attn_ref[rows, :] = jnp.concatenate([run_s[4, rows] / run_s[2, rows], run_s[5, rows] / run_s[3, rows]],
                                            axis=1).astype(BF16)


def _attn_prompt(x, g_pre, w_qkv_all, cos_t, sin_t):
    B = x.shape[0]
    W = GROUP_WIDTH
    keeps = [min(w, SEQ) for w, _ in ATTN_GROUPS]
    one = pl.Buffered(1)
    const = lambda shape: pl.BlockSpec(shape, lambda b: (0,) * len(shape), pipeline_mode=one)
    kv_specs, kv_shapes = [], []
    for keep in keeps:
        kv_specs += [pl.BlockSpec((None, W, keep), lambda b: (b, 0, 0))] * 2
        kv_shapes += [jax.ShapeDtypeStruct((B, W, keep), F32)] * 2
    outs = pl.pallas_call(
        _attn_prompt_kernel,
        grid=(B,),
        in_specs=[pl.BlockSpec((None, SEQ, D_MODEL), lambda b: (b, 0, 0), pipeline_mode=one),
                  const((1, D_MODEL)), const((D_MODEL, 9 * W)), const((SEQ, W)), const((SEQ, W))],
        out_specs=[pl.BlockSpec((None, SEQ, W), lambda b: (b, 0, 0))] + kv_specs,
        out_shape=[jax.ShapeDtypeStruct((B, SEQ, W), BF16)] + kv_shapes,
        scratch_shapes=[pltpu.VMEM((SEQ, D_MODEL), BF16),
                        pltpu.VMEM((6, SEQ, LANES), F32),
                        pltpu.VMEM((6, SEQ, LANES), F32)],
        compiler_params=pltpu.CompilerParams(dimension_semantics=("arbitrary",),
                                             vmem_limit_bytes=VMEM_LIMIT_V7X),
        name="attn_prompt",
    )(x, g_pre, w_qkv_all, cos_t, sin_t)
    kv = [a.reshape(B, HEADS, HEAD_DIM, a.shape[2]).transpose(0, 3, 1, 2)[None] for a in outs[1:]]
    return outs[0], kv


def _gelu_tanh(x):
    c = math.sqrt(2.0 / math.pi)
    u = x * (2.0 * c + (2.0 * c * 0.044715) * (x * x))
    return x / (1.0 + jnp.exp(-u))


def _dense_body(x, attn, wr, upbuf, *, roll_steps, u_hist, cnt_of, keep_u, up_hist, keep_up):
    (gpre, gpost, gfpre, gfpost, wpool, wgate, bgate, poolw, pscale, wbp, wba, wout,
     wupa, wupb, cwa, cwb, cba, cbb, wdown) = wr
    R = x.shape[0]
    h = _rms(x, gpre[...]).astype(BF16)

    u = jnp.dot(h, wpool[...], preferred_element_type=F32)
    keep_u(u)
    hist = u_hist()
    H = hist.shape[0]
    s = jnp.concatenate([hist, u], axis=0)
    ys = []
    for gi, w in enumerate(POOL_WINDOWS):
        c0 = gi * POOL_GROUP
        s = s[:, POOL_GROUP:] if gi else s
        s = s + roll_steps(s, w // 2)
        zmix = s[H:, :POOL_GROUP] / cnt_of(w) - u[:, c0:c0 + POOL_GROUP]
        ys.append(jnp.dot(zmix.astype(BF16), poolw[gi], preferred_element_type=F32))
    pool_out = jnp.concatenate(ys, axis=1) * pscale[...]
    a_br = jnp.dot(pool_out.astype(BF16), wbp[...], preferred_element_type=F32)

    b_br = jnp.dot(attn.astype(BF16), wba[...], preferred_element_type=F32)

    gates = jax.nn.sigmoid(jnp.dot(h, wgate[...], preferred_element_type=F32) + bgate[...])
    merged = gates[:, :D_MODEL] * a_br + gates[:, D_MODEL:] * b_br
    mix = jnp.dot(merged.astype(BF16), wout[...], preferred_element_type=F32)
    x1 = x + _rms(mix, gpost[...])

    hn = _rms(x1, gfpre[...]).astype(BF16)
    acc = jnp.zeros((R, D_MODEL), F32)

    def project(c):
        upbuf[c % 2, 0] = jnp.dot(hn, wupa[c], preferred_element_type=F32)
        upbuf[c % 2, 1] = jnp.dot(hn, wupb[c], preferred_element_type=F32)

    project(0)
    for c in range(NCH):
        if c + 1 < NCH:
            project(c + 1)
        ups = (upbuf[c % 2, 0], upbuf[c % 2, 1])
        keep_up(c, *ups)
        conv = []
        for half, (up, cw, cb) in enumerate(zip(ups, (cwa, cwb), (cba, cbb))):
            w = cw[c]
            hist_c = up_hist(c, half)
            hc = hist_c.shape[0]
            ext = jnp.concatenate([hist_c, up], axis=0)
            y = cb[c] + up * w[CONV_W - 1:CONV_W]
            for j in range(1, CONV_W):
                ext = roll_steps(ext, 1)
                y = y + ext[hc:] * w[CONV_W - 1 - j:CONV_W - j]
            conv.append(y)
        act = _gelu_tanh(conv[0]) * conv[1]
        acc = acc + jnp.dot(act.astype(BF16), wdown[c], preferred_element_type=F32)
    return x1 + _rms(acc, gfpost[...])


N_W = 19
HU_ROWS = 16
HC_ROWS = 8


def _dense_prompt_kernel(*refs):
    x_ref, attn_ref = refs[0:2]
    wr = refs[2:2 + N_W]
    y_ref, pool_ref, conv_ref = refs[2 + N_W:5 + N_W]
    uhist, chist, upbuf = refs[5 + N_W:]
    i = pl.program_id(1)

    @pl.when(i == 0)
    def _():
        uhist[...] = jnp.zeros(uhist.shape, F32)
        chist[...] = jnp.zeros(chist.shape, F32)

    pos = i * TS + lax.broadcasted_iota(jnp.int32, (TS, 1), 0)
    hist_u = uhist[...]

    def keep_u(u):
        uhist[...] = u[TS - HU_ROWS:]
        pool_ref[...] = u[TS - POOL_HIST:]

    def keep_up(c, ua, ub):
        for half, up in enumerate((ua, ub)):
            conv_ref[:, half * D_FF + c * FC:half * D_FF + (c + 1) * FC] = up[TS - (CONV_W - 1):]

    def up_hist(c, half):
        return chist[2 * c + half]

    def roll_steps(a, j):
        return pltpu.roll(a, j, 0)

    x = x_ref[...]
    wr_hist = []

    def keep_up_and_hist(c, ua, ub):
        keep_up(c, ua, ub)
        wr_hist.append((c, ua[TS - HC_ROWS:], ub[TS - HC_ROWS:]))

    y_ref[...] = _dense_body(x, attn_ref[...], wr, upbuf, roll_steps=roll_steps, u_hist=lambda: hist_u,
                             cnt_of=lambda w: jnp.minimum(pos + 1, w).astype(F32),
                             keep_u=keep_u, up_hist=up_hist, keep_up=keep_up_and_hist)
    for c, ta, tb in wr_hist:
        chist[2 * c] = ta
        chist[2 * c + 1] = tb


def _dense_sample_kernel(*refs, bs):
    x_ref, attn_ref, ph_ref, ch_ref = refs[0:4]
    wr = refs[4:4 + N_W]
    y_ref, u_ref, conv_ref = refs[4 + N_W:7 + N_W]
    upbuf = refs[7 + N_W]
    R = DEC_SEQ * bs
    HC = (CONV_W - 1) * bs

    def keep_u(u):
        u_ref[...] = u

    def keep_up(c, ua, ub):
        conv_ref[:, c * FC:(c + 1) * FC] = ua[R - HC:]
        conv_ref[:, D_FF + c * FC:D_FF + (c + 1) * FC] = ub[R - HC:]

    def up_hist(c, half):
        c0 = half * D_FF + c * FC
        return ch_ref[:, c0:c0 + FC]

    def roll_steps(a, j):
        return jnp.concatenate([a[a.shape[0] - j * bs:], a[:a.shape[0] - j * bs]], axis=0)

    y_ref[...] = _dense_body(x_ref[...], attn_ref[...], wr, upbuf, roll_steps=roll_steps, u_hist=lambda: ph_ref[...],
                             cnt_of=float, keep_u=keep_u, up_hist=up_hist, keep_up=keep_up)


def _weight_shapes():
    return [
        (1, D_MODEL), (1, D_MODEL), (1, D_MODEL), (1, D_MODEL),
        (D_MODEL, POOL_WIDTH), (D_MODEL, 2 * D_MODEL), (1, 2 * D_MODEL),
        (len(POOL_WINDOWS), POOL_GROUP, POOL_GROUP), (1, POOL_WIDTH),
        (POOL_WIDTH, D_MODEL), (GROUP_WIDTH, D_MODEL), (D_MODEL, D_MODEL),
        (NCH, D_MODEL, FC), (NCH, D_MODEL, FC),
        (NCH, CONV_W, FC), (NCH, CONV_W, FC),
        (NCH, 1, FC), (NCH, 1, FC),
        (NCH, FC, D_MODEL),
    ]


def _dense_prompt(x, attn, weights):
    B = x.shape[0]
    steps = SEQ // TS
    tile = lambda w: pl.BlockSpec((None, TS, w), lambda b, i: (b, i, 0))
    wspecs = [pl.BlockSpec(s, lambda b, i, n=len(s): (0,) * n, pipeline_mode=pl.Buffered(1)) for s in _weight_shapes()]
    y, pool_p, conv_p = pl.pallas_call(
        _dense_prompt_kernel,
        grid=(B, steps),
        in_specs=[tile(D_MODEL), tile(GROUP_WIDTH)] + wspecs,
        out_specs=[tile(D_MODEL),
                   pl.BlockSpec((None, POOL_HIST, POOL_WIDTH), lambda b, i: (b, 0, 0)),
                   pl.BlockSpec((None, CONV_W - 1, 2 * D_FF), lambda b, i: (b, 0, 0))],
        out_shape=[jax.ShapeDtypeStruct((B, SEQ, D_MODEL), F32),
                   jax.ShapeDtypeStruct((B, POOL_HIST, POOL_WIDTH), F32),
                   jax.ShapeDtypeStruct((B, CONV_W - 1, 2 * D_FF), F32)],
        scratch_shapes=[pltpu.VMEM((HU_ROWS, POOL_WIDTH), F32),
                        pltpu.VMEM((2 * NCH, HC_ROWS, FC), F32),
                        pltpu.VMEM((2, 2, TS, FC), F32)],
        compiler_params=pltpu.CompilerParams(dimension_semantics=("arbitrary", "arbitrary"),
                                             vmem_limit_bytes=VMEM_LIMIT_V7X),
        name="dense_prompt",
    )(x, attn, *weights)
    return y, pool_p, conv_p


def _dense_sample(x_tm, attn_tm, pool_hist_tm, conv_hist_tm, weights, bs):
    R = DEC_SEQ * bs
    full = lambda shape: pl.BlockSpec(shape, lambda n=len(shape): (0,) * n)
    y, u, conv_s = pl.pallas_call(
        functools.partial(_dense_sample_kernel, bs=bs),
        in_specs=[full((R, D_MODEL)), full((R, GROUP_WIDTH)),
                  full((POOL_HIST * bs, POOL_WIDTH)), full(((CONV_W - 1) * bs, 2 * D_FF))]
                 + [full(s) for s in _weight_shapes()],
        out_specs=[full((R, D_MODEL)), full((R, POOL_WIDTH)), full(((CONV_W - 1) * bs, 2 * D_FF))],
        out_shape=[jax.ShapeDtypeStruct((R, D_MODEL), F32),
                   jax.ShapeDtypeStruct((R, POOL_WIDTH), F32),
                   jax.ShapeDtypeStruct(((CONV_W - 1) * bs, 2 * D_FF), F32)],
        scratch_shapes=[pltpu.VMEM((2, 2, R, FC), F32)],
        compiler_params=pltpu.CompilerParams(vmem_limit_bytes=VMEM_LIMIT_V7X),
        name="dense_sample",
    )(x_tm, attn_tm, pool_hist_tm, conv_hist_tm, *weights)
    return y, u, conv_s


def _sample_qkv_kernel(x_ref, g_ref, w_ref, cos_ref, sin_ref, q_ref, k_ref, v_ref):
    W = GROUP_WIDTH
    h = _rms(x_ref[...], g_ref[...]).astype(BF16)
    qkv = jnp.dot(h, w_ref[...], preferred_element_type=F32)
    cos = cos_ref[...]
    sin = sin_ref[...]
    for g in range(len(ATTN_GROUPS)):
        b0 = 3 * W * g
        q_ref[:, g * W:(g + 1) * W] = _rope(qkv[:, b0:b0 + W], cos, sin) * QK_SCALE
        k_ref[:, g * W:(g + 1) * W] = _rope(qkv[:, b0 + W:b0 + 2 * W], cos, sin)
        v_ref[:, g * W:(g + 1) * W] = qkv[:, b0 + 2 * W:b0 + 3 * W]


def _sample_qkv(x_tm, g_pre, w_qkv_all, cos_s, sin_s):
    R = x_tm.shape[0]
    W = GROUP_WIDTH
    full = lambda shape: pl.BlockSpec(shape, lambda n=len(shape): (0,) * n)
    return pl.pallas_call(
        _sample_qkv_kernel,
        in_specs=[full((R, D_MODEL)), full((1, D_MODEL)), full((D_MODEL, 9 * W)), full((R, W)), full((R, W))],
        out_specs=[full((R, 3 * W))] * 3,
        out_shape=[jax.ShapeDtypeStruct((R, 3 * W), F32)] * 3,
        compiler_params=pltpu.CompilerParams(vmem_limit_bytes=VMEM_LIMIT_V7X),
        name="sample_qkv",
    )(x_tm, g_pre, w_qkv_all, cos_s, sin_s)


Q_ROWS = 16


def _lane_sum(a):
    return jnp.sum(a, axis=1, keepdims=True)


def _sample_attn_roll_kernel(*refs, bb):
    q_ref = refs[0]
    caches = refs[1:7]
    news = refs[7:13]
    attn_ref = refs[13]
    outs = refs[14:20]
    T = DEC_SEQ
    W = GROUP_WIDTH
    NEW0 = LANES - T

    lane_blk = lax.broadcasted_iota(jnp.int32, (bb * W, LANES), 1)
    for c_ref, n_ref, o_ref in zip(caches, news, outs):
        wb = c_ref.shape[-1]
        rolled = pltpu.roll(c_ref[...], wb - T, 1)
        if wb > LANES:
            o_ref[:, :wb - LANES] = rolled[:, :wb - LANES]
        o_ref[:, wb - LANES:] = jnp.where(lane_blk >= NEW0, n_ref[...], rolled[:, wb - LANES:])

    t_row = lax.broadcasted_iota(jnp.int32, (Q_ROWS, LANES), 0)
    lane_q = lax.broadcasted_iota(jnp.int32, (Q_ROWS, LANES), 1)
    lane1 = lax.broadcasted_iota(jnp.int32, (1, LANES), 1)
    lane_o = lax.broadcasted_iota(jnp.int32, (W, LANES), 1)
    for j in range(bb):
        q16 = q_ref[j].astype(BF16)
        per_group = []
        for g, (_, dil) in enumerate(ATTN_GROUPS):
            kc, vc, kn, vn = caches[2 * g], caches[2 * g + 1], news[2 * g], news[2 * g + 1]
            wb = kc.shape[-1]
            nt = wb // LANES
            qg = q16[:, g * W:(g + 1) * W]
            if dil == 1:
                pat = jnp.concatenate([jnp.where(t_row == t, 1.0, 0.0) for t in range(T)], axis=1)
            else:
                pat = jnp.concatenate([jnp.where((lane_q % dil == t_row) & (t_row < T), 1.0, 0.0),
                                       jnp.where((lane_q == NEW0 + t_row) & (t_row < T), 1.0, 0.0)], axis=1)
            qpat = lax.dot_general(qg, pat.astype(BF16), (((0,), (0,)), ((), ())),
                                   preferred_element_type=F32)
            res = [[None] * HEADS for _ in range(T)]
            for hh in range(HEADS):
                hr = slice(hh * HEAD_DIM, (hh + 1) * HEAD_DIM)
                rows = slice(j * W + hh * HEAD_DIM, j * W + (hh + 1) * HEAD_DIM)
                if dil == 1:
                    for t in range(T):
                        qp = qpat[hr, t * LANES:(t + 1) * LANES]
                        s = jnp.sum(kc[rows, :] * qp, axis=0, keepdims=True)
                        sn = jnp.sum(kn[rows, :] * qp, axis=0, keepdims=True)
                        ok_c = lane1 >= t
                        ok_n = (lane1 >= NEW0) & (lane1 <= NEW0 + t)
                        m = jnp.maximum(jnp.max(jnp.where(ok_c, s, NEG), axis=1, keepdims=True),
                                        jnp.max(jnp.where(ok_n, sn, NEG), axis=1, keepdims=True))
                        p = jnp.exp(jnp.where(ok_c, s - m, NEG))
                        pn = jnp.exp(jnp.where(ok_n, sn - m, NEG))
                        l = _lane_sum(p) + _lane_sum(pn)
                        num = (_lane_sum(vc[rows, :] * jnp.broadcast_to(p, (HEAD_DIM, LANES)))
                               + _lane_sum(vn[rows, :] * jnp.broadcast_to(pn, (HEAD_DIM, LANES))))
                        res[t][hh] = (m, l, num)
                else:
                    qp_c = qpat[hr, :LANES]
                    qp_n = qpat[hr, LANES:]
                    s = jnp.concatenate(
                        [jnp.sum(kc[rows, i * LANES:(i + 1) * LANES] * qp_c, axis=0, keepdims=True) for i in range(nt)],
                        axis=1)
                    sn = jnp.sum(kn[rows, :] * qp_n, axis=0, keepdims=True)
                    cls = lax.broadcasted_iota(jnp.int32, (1, wb), 1) % dil
                    ms = []
                    for t in range(T):
                        ms.append(jnp.maximum(jnp.max(jnp.where(cls == t, s, NEG), axis=1, keepdims=True),
                                              sn[:, NEW0 + t:NEW0 + t + 1]))
                    m_lane = jnp.where(cls == 0, ms[0], jnp.where(cls == 1, ms[1], jnp.where(cls == 2, ms[2], ms[3])))
                    p = jnp.exp(jnp.where(cls < T, s - m_lane, NEG))
                    pb = jnp.broadcast_to(p, (HEAD_DIM, wb))
                    acc = vc[rows, 0:LANES] * pb[:, 0:LANES]
                    for i in range(1, nt):
                        acc = acc + vc[rows, i * LANES:(i + 1) * LANES] * pb[:, i * LANES:(i + 1) * LANES]
                    cls1 = lane1 % dil
                    for t in range(T):
                        pn = jnp.exp(sn[:, NEW0 + t:NEW0 + t + 1] - ms[t])
                        l = _lane_sum(jnp.where(cls == t, p, 0.0)) + pn
                        num = (_lane_sum(jnp.where(cls1 == t, acc, 0.0))
                               + vn[rows, NEW0 + t:NEW0 + t + 1] * pn)
                        res[t][hh] = (ms[t], l, num)
            per_group.append(res)
        tile = jnp.zeros((W, LANES), F32)
        for t in range(T):
            cols = []
            for hh in range(HEADS):
                trip = [per_group[g][t][hh] for g in range(len(ATTN_GROUPS))]
                mx = jnp.maximum(jnp.maximum(trip[0][0], trip[1][0]), trip[2][0])
                den = 0.0
                num = 0.0
                for m, l, nm in trip:
                    e = jnp.exp(m - mx)
                    den = den + e * l
                    num = num + nm * e
                cols.append(num / den)
            tile = jnp.where(lane_o == t, jnp.concatenate(cols, axis=0), tile)
        attn_ref[j] = tile.T[0:SUBLANES]


def _sample_attn_roll(q_b, caches_t, news_t, bs, bb):
    W = GROUP_WIDTH
    cspecs = [pl.BlockSpec((bb * W, c.shape[1]), lambda i: (i, 0)) for c in caches_t]
    nspec = pl.BlockSpec((bb * W, LANES), lambda i: (i, 0))
    outs = pl.pallas_call(
        functools.partial(_sample_attn_roll_kernel, bb=bb),
        grid=(bs // bb,),
        in_specs=[pl.BlockSpec((bb, Q_ROWS, 3 * W), lambda i: (i, 0, 0))] + cspecs + [nspec] * len(caches_t),
        out_specs=[pl.BlockSpec((bb, SUBLANES, W), lambda i: (i, 0, 0))] + cspecs,
        out_shape=[jax.ShapeDtypeStruct((bs, SUBLANES, W), F32)]
                  + [jax.ShapeDtypeStruct(c.shape, c.dtype) for c in caches_t],
        compiler_params=pltpu.CompilerParams(dimension_semantics=("arbitrary",),
                                             vmem_limit_bytes=VMEM_LIMIT_V7X),
        name="sample_attn_roll",
    )(q_b, *caches_t, *news_t)
    return outs[0], outs[1:]


def _rope_tables(pos):
    inv_freq = ROPE_THETA ** (-jnp.arange(0, HEAD_DIM, 2, dtype=F32) / HEAD_DIM)
    ang = pos.astype(F32)[:, None] * inv_freq[None, :]
    cos = jnp.cos(ang)
    sin = jnp.sin(ang)
    cos_t = jnp.tile(jnp.concatenate([cos, cos], axis=-1), (1, HEADS))
    sin_t = jnp.tile(jnp.concatenate([-sin, sin], axis=-1), (1, HEADS))
    return cos_t, sin_t


def _layer_weights(l, norm_mix_pre, norm_mix_post, norm_ffn_pre, norm_ffn_post, w_in, b_gate, pool_w, pool_scale,
                   w_branch_pool, w_branch_attn, w_out, ffn_w_up, ffn_conv_w, ffn_conv_b, ffn_w_down):
    row = lambda a: a[l].reshape(1, -1)
    chunk_cols = lambda a: a.reshape(a.shape[0], NCH, FC).transpose(1, 0, 2)
    wi = w_in[l]
    up = ffn_w_up[l]
    cw = ffn_conv_w[l]
    cb = ffn_conv_b[l]
    dense = (
        row(norm_mix_pre), row(norm_mix_post), row(norm_ffn_pre), row(norm_ffn_post),
        wi[:, :POOL_WIDTH].astype(BF16), wi[:, GATE_OFF:].astype(BF16), row(b_gate),
        pool_w[l].astype(BF16), row(pool_scale),
        w_branch_pool[l].astype(BF16), w_branch_attn[l].astype(BF16), w_out[l].astype(BF16),
        chunk_cols(up[:, :D_FF]).astype(BF16), chunk_cols(up[:, D_FF:]).astype(BF16),
        chunk_cols(cw[:, :D_FF]), chunk_cols(cw[:, D_FF:]),
        cb[:D_FF].reshape(NCH, 1, FC), cb[D_FF:].reshape(NCH, 1, FC),
        ffn_w_down[l].reshape(NCH, FC, D_MODEL).astype(BF16),
    )
    w_qkv_all = wi[:, POOL_WIDTH:GATE_OFF].astype(BF16)
    return dense, w_qkv_all, row(norm_mix_pre)


def kernel(x_prompt, x_sample, state_pool, cache_k_w128, cache_v_w128, cache_k_w512, cache_v_w512, cache_k_w2048, cache_v_w2048, state_ffn_conv, norm_mix_pre, norm_mix_post, norm_ffn_pre, norm_ffn_post, w_in, b_gate, pool_w, pool_scale, w_branch_pool, w_branch_attn, w_out, ffn_w_up, ffn_conv_w, ffn_conv_b, ffn_w_down):
    depth = w_in.shape[0]
    assert depth == 1, "single-layer trunk"
    assert x_prompt.shape[1:] == (SEQ, D_MODEL) and x_sample.shape[1:] == (DEC_SEQ, D_MODEL)
    l = 0
    bs = x_sample.shape[0]
    W = GROUP_WIDTH
    dense_w, w_qkv_all, g_pre = _layer_weights(
        l, norm_mix_pre, norm_mix_post, norm_ffn_pre, norm_ffn_post, w_in, b_gate, pool_w, pool_scale,
        w_branch_pool, w_branch_attn, w_out, ffn_w_up, ffn_conv_w, ffn_conv_b, ffn_w_down)

    cos_p, sin_p = _rope_tables(jnp.arange(SEQ, dtype=jnp.int32))
    attn_p, kv_p = _attn_prompt(x_prompt, g_pre, w_qkv_all, cos_p, sin_p)
    y_p, pool_p, conv_p = _dense_prompt(x_prompt, attn_p, dense_w)

    cos_s, sin_s = _rope_tables(PAST_LEN + jnp.arange(DEC_SEQ, dtype=jnp.int32))
    cos_s = jnp.repeat(cos_s, bs, axis=0)
    sin_s = jnp.repeat(sin_s, bs, axis=0)
    x_tm = x_sample.transpose(1, 0, 2).reshape(DEC_SEQ * bs, D_MODEL)
    q_tm, k_tm, v_tm = _sample_qkv(x_tm, g_pre, w_qkv_all, cos_s, sin_s)
    q_b = jnp.pad(q_tm.reshape(DEC_SEQ, bs, 3 * W).transpose(1, 0, 2), ((0, 0), (0, Q_ROWS - DEC_SEQ), (0, 0)))
    cache_in = (cache_k_w128, cache_v_w128, cache_k_w512, cache_v_w512, cache_k_w2048, cache_v_w2048)
    caches_t = [c[l].transpose(0, 2, 3, 1).reshape(bs * W, c.shape[2]) for c in cache_in]
    news_t = []
    for g in range(len(ATTN_GROUPS)):
        for a in (k_tm, v_tm):
            new = a[:, g * W:(g + 1) * W].reshape(DEC_SEQ, bs, W).transpose(1, 2, 0).reshape(bs * W, DEC_SEQ)
            news_t.append(jnp.pad(new, ((0, 0), (LANES - DEC_SEQ, 0))))
    attn_b, rolled = _sample_attn_roll(q_b, caches_t, news_t, bs, bb=2)
    attn_s = attn_b[:, :DEC_SEQ].transpose(1, 0, 2).reshape(DEC_SEQ * bs, W)
    kv_s = [r.reshape(bs, HEADS, HEAD_DIM, r.shape[1]).transpose(0, 3, 1, 2)[None] for r in rolled]
    pool_hist_tm = state_pool[l].transpose(1, 0, 2).reshape(POOL_HIST * bs, POOL_WIDTH)
    conv_hist_tm = state_ffn_conv[l].transpose(1, 0, 2).reshape((CONV_W - 1) * bs, 2 * D_FF)
    y_tm, u_tm, conv_tm = _dense_sample(x_tm, attn_s, pool_hist_tm, conv_hist_tm, dense_w, bs)
    y_s = y_tm.reshape(DEC_SEQ, bs, D_MODEL).transpose(1, 0, 2)
    u_s = u_tm.reshape(DEC_SEQ, bs, POOL_WIDTH).transpose(1, 0, 2)
    pool_s = jnp.concatenate([state_pool[l][:, DEC_SEQ:], u_s], axis=1)
    conv_s = conv_tm.reshape(CONV_W - 1, bs, 2 * D_FF).transpose(1, 0, 2)

    return (y_p, y_s, pool_p[None], pool_s[None],
            kv_p[0], kv_s[0], kv_p[1], kv_s[1], kv_p[2], kv_s[2], kv_p[3], kv_s[3],
            kv_p[4], kv_s[4], kv_p[5], kv_s[5], conv_p[None], conv_s[None])
```

```python
import functools
import math

import jax
import jax.numpy as jnp
from jax import lax
from jax.experimental import pallas as pl
from jax.experimental.pallas import tpu as pltpu

F32 = jnp.float32
BF16 = jnp.bfloat16

D_MODEL = 1024
SEQ = 2048
DEC_SEQ = 4
PAST_LEN = 8192
POOL_WINDOWS = (2, 4, 8, 16)
POOL_WIDTH = 512
POOL_GROUP = 128
POOL_HIST = 15
ATTN_GROUPS = ((128, 1), (512, 4), (2048, 16))
HEAD_DIM = 64
HEADS = 4
GROUP_WIDTH = 256
ROPE_THETA = 10000.0
QB = 128
D_FF = 2816
CONV_W = 3
RMS_EPS = 1e-6
GATE_OFF = POOL_WIDTH + 3 * 3 * GROUP_WIDTH

LANES = 128
SUBLANES = 8
TS = 512
FC = 256
NCH = D_FF // FC
NEG = -1e30
QK_SCALE = 1.0 / math.sqrt(HEAD_DIM)
VMEM_LIMIT_V7X = 60000 * 1024


def _rms(x, g):
    ms = jnp.mean(x * x, axis=-1, keepdims=True)
    return x * lax.rsqrt(ms + RMS_EPS) * g


def _rope(x, cos, sin_signed):
    lane = lax.broadcasted_iota(jnp.int32, x.shape, 1)
    first_half = (lane % HEAD_DIM) < (HEAD_DIM // 2)
    swapped = jnp.where(first_half,
                        pltpu.roll(x, GROUP_WIDTH - HEAD_DIM // 2, 1),
                        pltpu.roll(x, HEAD_DIM // 2, 1))
    return x * cos + swapped * sin_signed


def _attn_prompt_kernel(x_ref, g_ref, w_ref, cos_ref, sin_ref,
                        attn_ref, k0_ref, v0_ref, k1_ref, v1_ref, k2_ref, v2_ref,
                        h_s, qkv_s, run_s):
    n_chunks = SEQ // TS
    kv_refs = ((k0_ref, v0_ref), (k1_ref, v1_ref), (k2_ref, v2_ref))
    W = GROUP_WIDTH
    for c in range(n_chunks):
        rows = slice(c * TS, (c + 1) * TS)
        h_s[rows] = _rms(x_ref[rows], g_ref[...]).astype(BF16)

    for g, (window, dil) in enumerate(ATTN_GROUPS):
        keep = min(window, SEQ)
        for c in range(n_chunks):
            rows = slice(c * TS, (c + 1) * TS)
            qkv = jnp.dot(h_s[rows], w_ref[:, 3 * W * g:3 * W * (g + 1)], preferred_element_type=F32)
            cos = cos_ref[rows]
            sin = sin_ref[rows]
            q = _rope(qkv[:, :W], cos, sin) * QK_SCALE
            k = _rope(qkv[:, W:2 * W], cos, sin)
            v = qkv[:, 2 * W:]
            for s, a in enumerate((q, k, v)):
                qkv_s[2 * s, rows] = a[:, :LANES]
                qkv_s[2 * s + 1, rows] = a[:, LANES:]
            lo = max(SEQ - keep, c * TS)
            if lo < (c + 1) * TS:
                r0 = lo - c * TS
                kv_refs[g][0][:, lo - (SEQ - keep):(c + 1) * TS - (SEQ - keep)] = k[r0:].T
                kv_refs[g][1][:, lo - (SEQ - keep):(c + 1) * TS - (SEQ - keep)] = v[r0:].T

        m_len = SEQ // dil
        nb = m_len // QB
        nk = 2 * QB if nb > 1 else QB

        def rows_of(start, n, dil=dil):
            return pl.ds(start, n, stride=dil) if dil > 1 else pl.ds(start, n)

        def block(n, carry, g=g, dil=dil, nb=nb, nk=nk, rows_of=rows_of):
            if nb > 1:
                r, jb = n // nb, n % nb
                off = jnp.where(jb > 0, QB, 0)
            else:
                r, jb, off = n, 0, 0
            start = r + dil * QB * jb
            wstart = start - dil * off
            lane_head = lax.broadcasted_iota(jnp.int32, (QB, W), 1) // HEAD_DIM
            lane_even = lax.broadcasted_iota(jnp.int32, (QB, LANES), 1) < HEAD_DIM
            diff = (lax.broadcasted_iota(jnp.int32, (QB, nk), 0) + off
                    - lax.broadcasted_iota(jnp.int32, (QB, nk), 1))
            bias = jnp.where((diff >= 0) & (diff <= QB), 0.0, NEG)
            qb = jnp.concatenate([qkv_s[0, rows_of(start, QB), :], qkv_s[1, rows_of(start, QB), :]], axis=1)
            kw = jnp.concatenate([qkv_s[2, rows_of(wstart, nk), :], qkv_s[3, rows_of(wstart, nk), :]],
                                 axis=1).astype(BF16)
            qm = jnp.concatenate([jnp.where(lane_head == hh, qb, 0.0) for hh in range(HEADS)],
                                 axis=0).astype(BF16)
            s = lax.dot_general(qm, kw, (((1,), (1,)), ((), ())), preferred_element_type=F32)
            s = s + jnp.concatenate([bias] * HEADS, axis=0)
            m = jnp.max(s, axis=-1, keepdims=True)
            p = jnp.exp(s - m)
            l = jnp.sum(p, axis=-1, keepdims=True)
            p16 = p.astype(BF16)
            for pr in range(2):
                vs = qkv_s[4 + pr, rows_of(wstart, nk), :].astype(BF16)
                h0 = slice(2 * pr * QB, (2 * pr + 1) * QB)
                h1 = slice((2 * pr + 1) * QB, (2 * pr + 2) * QB)
                acc = jnp.where(lane_even, jnp.dot(p16[h0], vs, preferred_element_type=F32),
                                jnp.dot(p16[h1], vs, preferred_element_type=F32))
                mm = jnp.where(lane_even, m[h0], m[h1])
                ll = jnp.where(lane_even, l[h0], l[h1])
                dst = rows_of(start, QB)
                if g == 0:
                    run_s[pr, dst, :] = mm
                    run_s[2 + pr, dst, :] = ll
                    run_s[4 + pr, dst, :] = acc
                else:
                    mo = run_s[pr, dst, :]
                    mn = jnp.maximum(mo, mm)
                    a = jnp.exp(mo - mn)
                    b = jnp.exp(mm - mn)
                    run_s[pr, dst, :] = mn
                    run_s[2 + pr, dst, :] = a * run_s[2 + pr, dst, :] + b * ll
                    run_s[4 + pr, dst, :] = a * run_s[4 + pr, dst, :] + b * acc
            return carry

        lax.fori_loop(0, SEQ // QB, block, 0, unroll=4)

    for c in range(n_chunks):
        rows = slice(c * TS, (c + 1) * TS)
        attn_ref[rows, :] = jnp.concatenate([run_s[4, rows] / run_s[2, rows], run_s[5, rows] / run_s[3, rows]],
                                            axis=1).astype(BF16)


def _attn_prompt(x, g_pre, w_qkv_all, cos_t, sin_t):
    B = x.shape[0]
    W = GROUP_WIDTH
    keeps = [min(w, SEQ) for w, _ in ATTN_GROUPS]
    one = pl.Buffered(1)
    const = lambda shape: pl.BlockSpec(shape, lambda b: (0,) * len(shape), pipeline_mode=one)
    kv_specs, kv_shapes = [], []
    for keep in keeps:
        kv_specs += [pl.BlockSpec((None, W, keep), lambda b: (b, 0, 0))] * 2
        kv_shapes += [jax.ShapeDtypeStruct((B, W, keep), F32)] * 2
    outs = pl.pallas_call(
        _attn_prompt_kernel,
        grid=(B,),
        in_specs=[pl.BlockSpec((None, SEQ, D_MODEL), lambda b: (b, 0, 0)),
                  const((1, D_MODEL)), const((D_MODEL, 9 * W)), const((SEQ, W)), const((SEQ, W))],
        out_specs=[pl.BlockSpec((None, SEQ, W), lambda b: (b, 0, 0))] + kv_specs,
        out_shape=[jax.ShapeDtypeStruct((B, SEQ, W), BF16)] + kv_shapes,
        scratch_shapes=[pltpu.VMEM((SEQ, D_MODEL), BF16),
                        pltpu.VMEM((6, SEQ, LANES), F32),
                        pltpu.VMEM((6, SEQ, LANES), F32)],
        compiler_params=pltpu.CompilerParams(dimension_semantics=("arbitrary",),
                                             vmem_limit_bytes=VMEM_LIMIT_V7X),
        name="attn_prompt",
    )(x, g_pre, w_qkv_all, cos_t, sin_t)
    kv = [a.reshape(B, HEADS, HEAD_DIM, a.shape[2]).transpose(0, 3, 1, 2)[None] for a in outs[1:]]
    return outs[0], kv


def _gelu_tanh(x):
    c = math.sqrt(2.0 / math.pi)
    u = x * (2.0 * c + (2.0 * c * 0.044715) * (x * x))
    return x / (1.0 + jnp.exp(-u))


def _dense_body(x, attn, wr, upbuf, actbuf, *, roll_steps, shift_in, u_hist, cnt_of, keep_u, up_hist, keep_up):
    (gpre, gpost, gfpre, gfpost, wpool, wgate, bgate, poolw, pscale, wbp, wba, wout,
     wupa, wupb, cwa, cwb, cba, cbb, wdown) = wr
    R = x.shape[0]
    h = _rms(x, gpre[...]).astype(BF16)

    u = jnp.dot(h, wpool[...], preferred_element_type=F32)
    keep_u(u)
    hist = u_hist()
    H = hist.shape[0]
    s = jnp.concatenate([hist, u], axis=0)
    ys = []
    for gi, w in enumerate(POOL_WINDOWS):
        c0 = gi * POOL_GROUP
        s = s[:, POOL_GROUP:] if gi else s
        s = s + roll_steps(s, w // 2)
        zmix = s[H:, :POOL_GROUP] / cnt_of(w) - u[:, c0:c0 + POOL_GROUP]
        ys.append(jnp.dot(zmix.astype(BF16), poolw[gi], preferred_element_type=F32))
    pool_out = jnp.concatenate(ys, axis=1) * pscale[...]
    a_br = jnp.dot(pool_out.astype(BF16), wbp[...], preferred_element_type=F32)

    b_br = jnp.dot(attn.astype(BF16), wba[...], preferred_element_type=F32)

    gates = jax.nn.sigmoid(jnp.dot(h, wgate[...], preferred_element_type=F32) + bgate[...])
    merged = gates[:, :D_MODEL] * a_br + gates[:, D_MODEL:] * b_br
    mix = jnp.dot(merged.astype(BF16), wout[...], preferred_element_type=F32)
    x1 = x + _rms(mix, gpost[...])

    hn = _rms(x1, gfpre[...]).astype(BF16)
    acc = jnp.zeros((R, D_MODEL), F32)

    def project(c):
        upbuf[c % 2, 0] = jnp.dot(hn, wupa[c], preferred_element_type=F32)
        upbuf[c % 2, 1] = jnp.dot(hn, wupb[c], preferred_element_type=F32)

    project(0)
    for c in range(NCH + 1):
        if c + 1 < NCH:
            project(c + 1)
        if c >= 1:
            acc = acc + jnp.dot(actbuf[(c - 1) % 2], wdown[c - 1], preferred_element_type=F32)
        if c == NCH:
            break
        ups = (upbuf[c % 2, 0], upbuf[c % 2, 1])
        keep_up(c, *ups)
        conv = []
        for half, (up, cw, cb) in enumerate(zip(ups, (cwa, cwb), (cba, cbb))):
            w = cw[c]
            hist_c = up_hist(c, half)
            y = cb[c] + up * w[CONV_W - 1:CONV_W]
            cur = up
            for j in range(1, CONV_W):
                cur = shift_in(cur, hist_c, j)
                y = y + cur * w[CONV_W - 1 - j:CONV_W - j]
            conv.append(y)
        actbuf[c % 2] = (_gelu_tanh(conv[0]) * conv[1]).astype(BF16)
    return x1 + _rms(acc, gfpost[...])


N_W = 19
HU_ROWS = 16
HC_ROWS = 8


def _dense_prompt_kernel(*refs):
    x_ref, attn_ref = refs[0:2]
    wr = refs[2:2 + N_W]
    y_ref, pool_ref, conv_ref = refs[2 + N_W:5 + N_W]
    uhist, chist, upbuf, actbuf = refs[5 + N_W:]
    i = pl.program_id(1)

    @pl.when(i == 0)
    def _():
        uhist[...] = jnp.zeros(uhist.shape, F32)
        chist[...] = jnp.zeros(chist.shape, F32)

    pos = i * TS + lax.broadcasted_iota(jnp.int32, (TS, 1), 0)
    hist_u = uhist[...]

    def keep_u(u):
        uhist[...] = u[TS - HU_ROWS:]
        pool_ref[...] = u[TS - POOL_HIST:]

    def keep_up(c, ua, ub):
        for half, up in enumerate((ua, ub)):
            conv_ref[:, half * D_FF + c * FC:half * D_FF + (c + 1) * FC] = up[TS - (CONV_W - 1):]

    def up_hist(c, half):
        return chist[2 * c + half]

    def roll_steps(a, j):
        return pltpu.roll(a, j, 0)

    row8 = lax.broadcasted_iota(jnp.int32, (SUBLANES, FC), 0)

    def shift_in(a, hist, j):
        r = pltpu.roll(a, 1, 0)
        top = jnp.where(row8 == 0, hist[HC_ROWS - j:HC_ROWS - j + 1], r[0:SUBLANES])
        return jnp.concatenate([top, r[SUBLANES:]], axis=0)

    x = x_ref[...]
    wr_hist = []

    def keep_up_and_hist(c, ua, ub):
        keep_up(c, ua, ub)
        wr_hist.append((c, ua[TS - HC_ROWS:], ub[TS - HC_ROWS:]))

    y_ref[...] = _dense_body(x, attn_ref[...], wr, upbuf, actbuf, roll_steps=roll_steps, shift_in=shift_in, u_hist=lambda: hist_u,
                             cnt_of=lambda w: jnp.minimum(pos + 1, w).astype(F32),
                             keep_u=keep_u, up_hist=up_hist, keep_up=keep_up_and_hist)
    for c, ta, tb in wr_hist:
        chist[2 * c] = ta
        chist[2 * c + 1] = tb


def _dense_sample_kernel(*refs, bs):
    x_ref, attn_ref, ph_ref, ch_ref = refs[0:4]
    wr = refs[4:4 + N_W]
    y_ref, u_ref, conv_ref = refs[4 + N_W:7 + N_W]
    upbuf, actbuf = refs[7 + N_W:]
    R = DEC_SEQ * bs
    HC = (CONV_W - 1) * bs

    def keep_u(u):
        u_ref[...] = u

    def keep_up(c, ua, ub):
        conv_ref[:, c * FC:(c + 1) * FC] = ua[R - HC:]
        conv_ref[:, D_FF + c * FC:D_FF + (c + 1) * FC] = ub[R - HC:]

    def up_hist(c, half):
        c0 = half * D_FF + c * FC
        return ch_ref[:, c0:c0 + FC]

    def roll_steps(a, j):
        return jnp.concatenate([a[a.shape[0] - j * bs:], a[:a.shape[0] - j * bs]], axis=0)

    def shift_in(a, hist, j):
        return jnp.concatenate([hist[HC - j * bs:HC - (j - 1) * bs], a[:R - bs]], axis=0)

    y_ref[...] = _dense_body(x_ref[...], attn_ref[...], wr, upbuf, actbuf, roll_steps=roll_steps, shift_in=shift_in, u_hist=lambda: ph_ref[...],
                             cnt_of=float, keep_u=keep_u, up_hist=up_hist, keep_up=keep_up)


def _weight_shapes():
    return [
        (1, D_MODEL), (1, D_MODEL), (1, D_MODEL), (1, D_MODEL),
        (D_MODEL, POOL_WIDTH), (D_MODEL, 2 * D_MODEL), (1, 2 * D_MODEL),
        (len(POOL_WINDOWS), POOL_GROUP, POOL_GROUP), (1, POOL_WIDTH),
        (POOL_WIDTH, D_MODEL), (GROUP_WIDTH, D_MODEL), (D_MODEL, D_MODEL),
        (NCH, D_MODEL, FC), (NCH, D_MODEL, FC),
        (NCH, CONV_W, FC), (NCH, CONV_W, FC),
        (NCH, 1, FC), (NCH, 1, FC),
        (NCH, FC, D_MODEL),
    ]


def _dense_prompt(x, attn, weights):
    B = x.shape[0]
    steps = SEQ // TS
    tile = lambda w: pl.BlockSpec((None, TS, w), lambda b, i: (b, i, 0))
    wspecs = [pl.BlockSpec(s, lambda b, i, n=len(s): (0,) * n, pipeline_mode=pl.Buffered(1)) for s in _weight_shapes()]
    y, pool_p, conv_p = pl.pallas_call(
        _dense_prompt_kernel,
        grid=(B, steps),
        in_specs=[tile(D_MODEL), tile(GROUP_WIDTH)] + wspecs,
        out_specs=[tile(D_MODEL),
                   pl.BlockSpec((None, POOL_HIST, POOL_WIDTH), lambda b, i: (b, 0, 0)),
                   pl.BlockSpec((None, CONV_W - 1, 2 * D_FF), lambda b, i: (b, 0, 0))],
        out_shape=[jax.ShapeDtypeStruct((B, SEQ, D_MODEL), F32),
                   jax.ShapeDtypeStruct((B, POOL_HIST, POOL_WIDTH), F32),
                   jax.ShapeDtypeStruct((B, CONV_W - 1, 2 * D_FF), F32)],
        scratch_shapes=[pltpu.VMEM((HU_ROWS, POOL_WIDTH), F32),
                        pltpu.VMEM((2 * NCH, HC_ROWS, FC), F32),
                        pltpu.VMEM((2, 2, TS, FC), F32),
                        pltpu.VMEM((2, TS, FC), BF16)],
        compiler_params=pltpu.CompilerParams(dimension_semantics=("arbitrary", "arbitrary"),
                                             vmem_limit_bytes=VMEM_LIMIT_V7X),
        name="dense_prompt",
    )(x, attn, *weights)
    return y, pool_p, conv_p


def _dense_sample(x_tm, attn_tm, pool_hist_tm, conv_hist_tm, weights, bs):
    R = DEC_SEQ * bs
    full = lambda shape: pl.BlockSpec(shape, lambda n=len(shape): (0,) * n)
    y, u, conv_s = pl.pallas_call(
        functools.partial(_dense_sample_kernel, bs=bs),
        in_specs=[full((R, D_MODEL)), full((R, GROUP_WIDTH)),
                  full((POOL_HIST * bs, POOL_WIDTH)), full(((CONV_W - 1) * bs, 2 * D_FF))]
                 + [full(s) for s in _weight_shapes()],
        out_specs=[full((R, D_MODEL)), full((R, POOL_WIDTH)), full(((CONV_W - 1) * bs, 2 * D_FF))],
        out_shape=[jax.ShapeDtypeStruct((R, D_MODEL), F32),
                   jax.ShapeDtypeStruct((R, POOL_WIDTH), F32),
                   jax.ShapeDtypeStruct(((CONV_W - 1) * bs, 2 * D_FF), F32)],
        scratch_shapes=[pltpu.VMEM((2, 2, R, FC), F32), pltpu.VMEM((2, R, FC), BF16)],
        compiler_params=pltpu.CompilerParams(vmem_limit_bytes=VMEM_LIMIT_V7X),
        name="dense_sample",
    )(x_tm, attn_tm, pool_hist_tm, conv_hist_tm, *weights)
    return y, u, conv_s


def _sample_qkv_kernel(x_ref, g_ref, w_ref, cos_ref, sin_ref, q_ref, k_ref, v_ref):
    W = GROUP_WIDTH
    h = _rms(x_ref[...], g_ref[...]).astype(BF16)
    qkv = jnp.dot(h, w_ref[...], preferred_element_type=F32)
    cos = cos_ref[...]
    sin = sin_ref[...]
    for g in range(len(ATTN_GROUPS)):
        b0 = 3 * W * g
        q_ref[:, g * W:(g + 1) * W] = _rope(qkv[:, b0:b0 + W], cos, sin) * QK_SCALE
        k_ref[:, g * W:(g + 1) * W] = _rope(qkv[:, b0 + W:b0 + 2 * W], cos, sin)
        v_ref[:, g * W:(g + 1) * W] = qkv[:, b0 + 2 * W:b0 + 3 * W]


def _sample_qkv(x_tm, g_pre, w_qkv_all, cos_s, sin_s):
    R = x_tm.shape[0]
    W = GROUP_WIDTH
    full = lambda shape: pl.BlockSpec(shape, lambda n=len(shape): (0,) * n)
    return pl.pallas_call(
        _sample_qkv_kernel,
        in_specs=[full((R, D_MODEL)), full((1, D_MODEL)), full((D_MODEL, 9 * W)), full((R, W)), full((R, W))],
        out_specs=[full((R, 3 * W))] * 3,
        out_shape=[jax.ShapeDtypeStruct((R, 3 * W), F32)] * 3,
        compiler_params=pltpu.CompilerParams(vmem_limit_bytes=VMEM_LIMIT_V7X),
        name="sample_qkv",
    )(x_tm, g_pre, w_qkv_all, cos_s, sin_s)


Q_ROWS = 16


def _lane_sum(a):
    return jnp.sum(a, axis=1, keepdims=True)


def _fold_rows(a):
    out = a[0:SUBLANES]
    for i in range(1, a.shape[0] // SUBLANES):
        out = out + a[i * SUBLANES:(i + 1) * SUBLANES]
    return out


def _sample_attn_roll_kernel(*refs, bb):
    q_ref = refs[0]
    caches = refs[1:7]
    news = refs[7:13]
    attn_ref = refs[13]
    outs = refs[14:20]
    T = DEC_SEQ
    W = GROUP_WIDTH
    NEW0 = LANES - T

    lane_blk = lax.broadcasted_iota(jnp.int32, (bb * W, LANES), 1)
    for c_ref, n_ref, o_ref in zip(caches, news, outs):
        wb = c_ref.shape[-1]
        rolled = pltpu.roll(c_ref[...], wb - T, 1)
        if wb > LANES:
            o_ref[:, :wb - LANES] = rolled[:, :wb - LANES]
        o_ref[:, wb - LANES:] = jnp.where(lane_blk >= NEW0, n_ref[...], rolled[:, wb - LANES:])

    t_row = lax.broadcasted_iota(jnp.int32, (Q_ROWS, LANES), 0)
    lane_q = lax.broadcasted_iota(jnp.int32, (Q_ROWS, LANES), 1)
    lane1 = lax.broadcasted_iota(jnp.int32, (1, LANES), 1)
    lane_o = lax.broadcasted_iota(jnp.int32, (W, LANES), 1)
    for j in range(bb):
        q16 = q_ref[j].astype(BF16)
        per_group = []
        for g, (_, dil) in enumerate(ATTN_GROUPS):
            kc, vc, kn, vn = caches[2 * g], caches[2 * g + 1], news[2 * g], news[2 * g + 1]
            wb = kc.shape[-1]
            nt = wb // LANES
            qg = q16[:, g * W:(g + 1) * W]
            if dil == 1:
                pat = jnp.concatenate([jnp.where(t_row == t, 1.0, 0.0) for t in range(T)], axis=1)
            else:
                pat = jnp.concatenate([jnp.where((lane_q % dil == t_row) & (t_row < T), 1.0, 0.0),
                                       jnp.where((lane_q == NEW0 + t_row) & (t_row < T), 1.0, 0.0)], axis=1)
            qpat = lax.dot_general(qg, pat.astype(BF16), (((0,), (0,)), ((), ())),
                                   preferred_element_type=F32)
            res = [[None] * HEADS for _ in range(T)]
            hrs = [slice(hh * HEAD_DIM, (hh + 1) * HEAD_DIM) for hh in range(HEADS)]
            rws = [slice(j * W + hh * HEAD_DIM, j * W + (hh + 1) * HEAD_DIM) for hh in range(HEADS)]

            def heads_on_rows(per_head):
                n = per_head[0].shape[1]
                row = lax.broadcasted_iota(jnp.int32, (SUBLANES, n), 0)
                out = jnp.full((SUBLANES, n), NEG, F32)
                for hh in range(HEADS):
                    out = jnp.where(row == hh, per_head[hh], out)
                return out

            if dil == 1:
                for t in range(T):
                    qps = [qpat[hrs[hh], t * LANES:(t + 1) * LANES] for hh in range(HEADS)]
                    s = heads_on_rows([jnp.sum(_fold_rows(kc[rws[hh], :] * qps[hh]), axis=0, keepdims=True)
                                       for hh in range(HEADS)])
                    sn = heads_on_rows([jnp.sum(_fold_rows(kn[rws[hh], :] * qps[hh]), axis=0, keepdims=True)
                                        for hh in range(HEADS)])
                    ok_c = lane1 >= t
                    ok_n = (lane1 >= NEW0) & (lane1 <= NEW0 + t)
                    m = jnp.maximum(jnp.max(jnp.where(ok_c, s, NEG), axis=1, keepdims=True),
                                    jnp.max(jnp.where(ok_n, sn, NEG), axis=1, keepdims=True))
                    p = jnp.exp(jnp.where(ok_c, s - m, NEG))
                    pn = jnp.exp(jnp.where(ok_n, sn - m, NEG))
                    l = _lane_sum(p) + _lane_sum(pn)
                    for hh in range(HEADS):
                        num = (_lane_sum(vc[rws[hh], :] * jnp.broadcast_to(p[hh:hh + 1], (HEAD_DIM, LANES)))
                               + _lane_sum(vn[rws[hh], :] * jnp.broadcast_to(pn[hh:hh + 1], (HEAD_DIM, LANES))))
                        res[t][hh] = (m[hh:hh + 1], l[hh:hh + 1], num)
            else:
                s = heads_on_rows([jnp.sum(jnp.concatenate(
                    [_fold_rows(kc[rws[hh], i * LANES:(i + 1) * LANES] * qpat[hrs[hh], :LANES]) for i in range(nt)],
                    axis=1), axis=0, keepdims=True) for hh in range(HEADS)])
                sn = heads_on_rows([jnp.sum(_fold_rows(kn[rws[hh], :] * qpat[hrs[hh], LANES:]), axis=0, keepdims=True)
                                    for hh in range(HEADS)])
                cls = lax.broadcasted_iota(jnp.int32, (1, wb), 1) % dil
                ms = []
                for t in range(T):
                    ms.append(jnp.maximum(jnp.max(jnp.where(cls == t, s, NEG), axis=1, keepdims=True),
                                          sn[:, NEW0 + t:NEW0 + t + 1]))
                m_lane = jnp.where(cls == 0, ms[0], jnp.where(cls == 1, ms[1], jnp.where(cls == 2, ms[2], ms[3])))
                p = jnp.exp(jnp.where(cls < T, s - m_lane, NEG))
                pns = [jnp.exp(sn[:, NEW0 + t:NEW0 + t + 1] - ms[t]) for t in range(T)]
                ls = [_lane_sum(jnp.where(cls == t, p, 0.0)) + pns[t] for t in range(T)]
                cls1 = lane1 % dil
                for hh in range(HEADS):
                    pb = jnp.broadcast_to(p[hh:hh + 1], (HEAD_DIM, wb))
                    acc = vc[rws[hh], 0:LANES] * pb[:, 0:LANES]
                    for i in range(1, nt):
                        acc = acc + vc[rws[hh], i * LANES:(i + 1) * LANES] * pb[:, i * LANES:(i + 1) * LANES]
                    for t in range(T):
                        num = (_lane_sum(jnp.where(cls1 == t, acc, 0.0))
                               + vn[rws[hh], NEW0 + t:NEW0 + t + 1] * pns[t][hh:hh + 1])
                        res[t][hh] = (ms[t][hh:hh + 1], ls[t][hh:hh + 1], num)
            per_group.append(res)
        tile = jnp.zeros((W, LANES), F32)
        for t in range(T):
            cols = []
            for hh in range(HEADS):
                trip = [per_group[g][t][hh] for g in range(len(ATTN_GROUPS))]
                mx = jnp.maximum(jnp.maximum(trip[0][0], trip[1][0]), trip[2][0])
                den = 0.0
                num = 0.0
                for m, l, nm in trip:
                    e = jnp.exp(m - mx)
                    den = den + e * l
                    num = num + nm * e
                cols.append(num / den)
            tile = jnp.where(lane_o == t, jnp.concatenate(cols, axis=0), tile)
        attn_ref[j] = tile.T[0:SUBLANES]


def _sample_attn_roll(q_b, caches_t, news_t, bs, bb):
    W = GROUP_WIDTH
    cspecs = [pl.BlockSpec((bb * W, c.shape[1]), lambda i: (i, 0)) for c in caches_t]
    nspec = pl.BlockSpec((bb * W, LANES), lambda i: (i, 0))
    outs = pl.pallas_call(
        functools.partial(_sample_attn_roll_kernel, bb=bb),
        grid=(bs // bb,),
        in_specs=[pl.BlockSpec((bb, Q_ROWS, 3 * W), lambda i: (i, 0, 0))] + cspecs + [nspec] * len(caches_t),
        out_specs=[pl.BlockSpec((bb, SUBLANES, W), lambda i: (i, 0, 0))] + cspecs,
        out_shape=[jax.ShapeDtypeStruct((bs, SUBLANES, W), F32)]
                  + [jax.ShapeDtypeStruct(c.shape, c.dtype) for c in caches_t],
        compiler_params=pltpu.CompilerParams(dimension_semantics=("arbitrary",),
                                             vmem_limit_bytes=VMEM_LIMIT_V7X),
        name="sample_attn_roll",
    )(q_b, *caches_t, *news_t)
    return outs[0], outs[1:]


def _rope_tables(pos):
    inv_freq = ROPE_THETA ** (-jnp.arange(0, HEAD_DIM, 2, dtype=F32) / HEAD_DIM)
    ang = pos.astype(F32)[:, None] * inv_freq[None, :]
    cos = jnp.cos(ang)
    sin = jnp.sin(ang)
    cos_t = jnp.tile(jnp.concatenate([cos, cos], axis=-1), (1, HEADS))
    sin_t = jnp.tile(jnp.concatenate([-sin, sin], axis=-1), (1, HEADS))
    return cos_t, sin_t


def _layer_weights(l, norm_mix_pre, norm_mix_post, norm_ffn_pre, norm_ffn_post, w_in, b_gate, pool_w, pool_scale,
                   w_branch_pool, w_branch_attn, w_out, ffn_w_up, ffn_conv_w, ffn_conv_b, ffn_w_down):
    row = lambda a: a[l].reshape(1, -1)
    chunk_cols = lambda a: a.reshape(a.shape[0], NCH, FC).transpose(1, 0, 2)
    wi = w_in[l]
    up = ffn_w_up[l]
    cw = ffn_conv_w[l]
    cb = ffn_conv_b[l]
    dense = (
        row(norm_mix_pre), row(norm_mix_post), row(norm_ffn_pre), row(norm_ffn_post),
        wi[:, :POOL_WIDTH].astype(BF16), wi[:, GATE_OFF:].astype(BF16), row(b_gate),
        pool_w[l].astype(BF16), row(pool_scale),
        w_branch_pool[l].astype(BF16), w_branch_attn[l].astype(BF16), w_out[l].astype(BF16),
        chunk_cols(up[:, :D_FF]).astype(BF16), chunk_cols(up[:, D_FF:]).astype(BF16),
        chunk_cols(cw[:, :D_FF]), chunk_cols(cw[:, D_FF:]),
        cb[:D_FF].reshape(NCH, 1, FC), cb[D_FF:].reshape(NCH, 1, FC),
        ffn_w_down[l].reshape(NCH, FC, D_MODEL).astype(BF16),
    )
    w_qkv_all = wi[:, POOL_WIDTH:GATE_OFF].astype(BF16)
    return dense, w_qkv_all, row(norm_mix_pre)


def kernel(x_prompt, x_sample, state_pool, cache_k_w128, cache_v_w128, cache_k_w512, cache_v_w512, cache_k_w2048, cache_v_w2048, state_ffn_conv, norm_mix_pre, norm_mix_post, norm_ffn_pre, norm_ffn_post, w_in, b_gate, pool_w, pool_scale, w_branch_pool, w_branch_attn, w_out, ffn_w_up, ffn_conv_w, ffn_conv_b, ffn_w_down):
    depth = w_in.shape[0]
    assert depth == 1, "single-layer trunk"
    assert x_prompt.shape[1:] == (SEQ, D_MODEL) and x_sample.shape[1:] == (DEC_SEQ, D_MODEL)
    l = 0
    bs = x_sample.shape[0]
    W = GROUP_WIDTH
    dense_w, w_qkv_all, g_pre = _layer_weights(
        l, norm_mix_pre, norm_mix_post, norm_ffn_pre, norm_ffn_post, w_in, b_gate, pool_w, pool_scale,
        w_branch_pool, w_branch_attn, w_out, ffn_w_up, ffn_conv_w, ffn_conv_b, ffn_w_down)

    cos_p, sin_p = _rope_tables(jnp.arange(SEQ, dtype=jnp.int32))
    attn_p, kv_p = _attn_prompt(x_prompt, g_pre, w_qkv_all, cos_p, sin_p)
    y_p, pool_p, conv_p = _dense_prompt(x_prompt, attn_p, dense_w)

    cos_s, sin_s = _rope_tables(PAST_LEN + jnp.arange(DEC_SEQ, dtype=jnp.int32))
    cos_s = jnp.repeat(cos_s, bs, axis=0)
    sin_s = jnp.repeat(sin_s, bs, axis=0)
    x_tm = x_sample.transpose(1, 0, 2).reshape(DEC_SEQ * bs, D_MODEL)
    q_tm, k_tm, v_tm = _sample_qkv(x_tm, g_pre, w_qkv_all, cos_s, sin_s)
    q_b = jnp.pad(q_tm.reshape(DEC_SEQ, bs, 3 * W).transpose(1, 0, 2), ((0, 0), (0, Q_ROWS - DEC_SEQ), (0, 0)))
    cache_in = (cache_k_w128, cache_v_w128, cache_k_w512, cache_v_w512, cache_k_w2048, cache_v_w2048)
    caches_t = [c[l].transpose(0, 2, 3, 1).reshape(bs * W, c.shape[2]) for c in cache_in]
    news_t = []
    for g in range(len(ATTN_GROUPS)):
        for a in (k_tm, v_tm):
            new = a[:, g * W:(g + 1) * W].reshape(DEC_SEQ, bs, W).transpose(1, 2, 0).reshape(bs * W, DEC_SEQ)
            news_t.append(jnp.pad(new, ((0, 0), (LANES - DEC_SEQ, 0))))
    attn_b, rolled = _sample_attn_roll(q_b, caches_t, news_t, bs, bb=2)
    attn_s = attn_b[:, :DEC_SEQ].transpose(1, 0, 2).reshape(DEC_SEQ * bs, W)
    kv_s = [r.reshape(bs, HEADS, HEAD_DIM, r.shape[1]).transpose(0, 3, 1, 2)[None] for r in rolled]
    pool_hist_tm = state_pool[l].transpose(1, 0, 2).reshape(POOL_HIST * bs, POOL_WIDTH)
    conv_hist_tm = state_ffn_conv[l].transpose(1, 0, 2).reshape((CONV_W - 1) * bs, 2 * D_FF)
    y_tm, u_tm, conv_tm = _dense_sample(x_tm, attn_s, pool_hist_tm, conv_hist_tm, dense_w, bs)
    y_s = y_tm.reshape(DEC_SEQ, bs, D_MODEL).transpose(1, 0, 2)
    u_s = u_tm.reshape(DEC_SEQ, bs, POOL_WIDTH).transpose(1, 0, 2)
    pool_s = jnp.concatenate([state_pool[l][:, DEC_SEQ:], u_s], axis=1)
    conv_s = conv_tm.reshape(CONV_W - 1, bs, 2 * D_FF).transpose(1, 0, 2)

    return (y_p, y_s, pool_p[None], pool_s[None],
            kv_p[0], kv_s[0], kv_p[1], kv_s[1], kv_p[2], kv_s[2], kv_p[3], kv_s[3],
            kv_p[4], kv_s[4], kv_p[5], kv_s[5], conv_p[None], conv_s[None])
```

```python
import functools
import math

import jax
import jax.numpy as jnp
from jax import lax
from jax.experimental import pallas as pl
from jax.experimental.pallas import tpu as pltpu

F32 = jnp.float32
BF16 = jnp.bfloat16

D_MODEL = 1024
SEQ = 2048
DEC_SEQ = 4
PAST_LEN = 8192
POOL_WINDOWS = (2, 4, 8, 16)
POOL_WIDTH = 512
POOL_GROUP = 128
POOL_HIST = 15
ATTN_GROUPS = ((128, 1), (512, 4), (2048, 16))
HEAD_DIM = 64
HEADS = 4
GROUP_WIDTH = 256
ROPE_THETA = 10000.0
QB = 128
D_FF = 2816
CONV_W = 3
RMS_EPS = 1e-6
GATE_OFF = POOL_WIDTH + 3 * 3 * GROUP_WIDTH

LANES = 128
SUBLANES = 8
TS = 512
FC = 256
NCH = D_FF // FC
N_BUF = 2
NEG = -1e30
QK_SCALE = 1.0 / math.sqrt(HEAD_DIM)
VMEM_LIMIT_V7X = 60000 * 1024


def _rms(x, g):
    ms = jnp.mean(x * x, axis=-1, keepdims=True)
    return x * lax.rsqrt(ms + RMS_EPS) * g


def _rope(x, cos, sin_signed):
    lane = lax.broadcasted_iota(jnp.int32, x.shape, 1)
    first_half = (lane % HEAD_DIM) < (HEAD_DIM // 2)
    swapped = jnp.where(first_half,
                        pltpu.roll(x, GROUP_WIDTH - HEAD_DIM // 2, 1),
                        pltpu.roll(x, HEAD_DIM // 2, 1))
    return x * cos + swapped * sin_signed


def _attn_prompt_kernel(x_ref, g_ref, w_ref, cos_ref, sin_ref,
                        attn_ref, k0_ref, v0_ref, k1_ref, v1_ref, k2_ref, v2_ref,
                        h_s, qkv_s, run_s):
    n_chunks = SEQ // TS
    kv_refs = ((k0_ref, v0_ref), (k1_ref, v1_ref), (k2_ref, v2_ref))
    W = GROUP_WIDTH
    for c in range(n_chunks):
        rows = slice(c * TS, (c + 1) * TS)
        h_s[rows] = _rms(x_ref[rows], g_ref[...]).astype(BF16)

    for g, (window, dil) in enumerate(ATTN_GROUPS):
        keep = min(window, SEQ)
        for c in range(n_chunks):
            rows = slice(c * TS, (c + 1) * TS)
            qkv = jnp.dot(h_s[rows], w_ref[:, 3 * W * g:3 * W * (g + 1)], preferred_element_type=F32)
            cos = cos_ref[rows]
            sin = sin_ref[rows]
            q = _rope(qkv[:, :W], cos, sin) * QK_SCALE
            k = _rope(qkv[:, W:2 * W], cos, sin)
            v = qkv[:, 2 * W:]
            for s, a in enumerate((q, k, v)):
                qkv_s[2 * s, rows] = a[:, :LANES]
                qkv_s[2 * s + 1, rows] = a[:, LANES:]
            lo = max(SEQ - keep, c * TS)
            if lo < (c + 1) * TS:
                r0 = lo - c * TS
                kv_refs[g][0][:, lo - (SEQ - keep):(c + 1) * TS - (SEQ - keep)] = k[r0:].T
                kv_refs[g][1][:, lo - (SEQ - keep):(c + 1) * TS - (SEQ - keep)] = v[r0:].T

        m_len = SEQ // dil
        nb = m_len // QB
        nk = 2 * QB if nb > 1 else QB

        def rows_of(start, n, dil=dil):
            return pl.ds(start, n, stride=dil) if dil > 1 else pl.ds(start, n)

        def block(n, carry, g=g, dil=dil, nb=nb, nk=nk, rows_of=rows_of):
            if nb > 1:
                r, jb = n // nb, n % nb
                off = jnp.where(jb > 0, QB, 0)
            else:
                r, jb, off = n, 0, 0
            start = r + dil * QB * jb
            wstart = start - dil * off
            lane_head = lax.broadcasted_iota(jnp.int32, (QB, W), 1) // HEAD_DIM
            lane_even = lax.broadcasted_iota(jnp.int32, (QB, LANES), 1) < HEAD_DIM
            diff = (lax.broadcasted_iota(jnp.int32, (QB, nk), 0) + off
                    - lax.broadcasted_iota(jnp.int32, (QB, nk), 1))
            bias = jnp.where((diff >= 0) & (diff <= QB), 0.0, NEG)
            qb = jnp.concatenate([qkv_s[0, rows_of(start, QB), :], qkv_s[1, rows_of(start, QB), :]], axis=1)
            kw = jnp.concatenate([qkv_s[2, rows_of(wstart, nk), :], qkv_s[3, rows_of(wstart, nk), :]],
                                 axis=1).astype(BF16)
            qm = jnp.concatenate([jnp.where(lane_head == hh, qb, 0.0) for hh in range(HEADS)],
                                 axis=0).astype(BF16)
            s = lax.dot_general(qm, kw, (((1,), (1,)), ((), ())), preferred_element_type=F32)
            s = s + jnp.concatenate([bias] * HEADS, axis=0)
            m = jnp.max(s, axis=-1, keepdims=True)
            p = jnp.exp(s - m)
            l = jnp.sum(p, axis=-1, keepdims=True)
            p16 = p.astype(BF16)
            for pr in range(2):
                vs = qkv_s[4 + pr, rows_of(wstart, nk), :].astype(BF16)
                h0 = slice(2 * pr * QB, (2 * pr + 1) * QB)
                h1 = slice((2 * pr + 1) * QB, (2 * pr + 2) * QB)
                acc = jnp.where(lane_even, jnp.dot(p16[h0], vs, preferred_element_type=F32),
                                jnp.dot(p16[h1], vs, preferred_element_type=F32))
                mm = jnp.where(lane_even, m[h0], m[h1])
                ll = jnp.where(lane_even, l[h0], l[h1])
                dst = rows_of(start, QB)
                if g == 0:
                    run_s[pr, dst, :] = mm
                    run_s[2 + pr, dst, :] = ll
                    run_s[4 + pr, dst, :] = acc
                else:
                    mo = run_s[pr, dst, :]
                    mn = jnp.maximum(mo, mm)
                    a = jnp.exp(mo - mn)
                    b = jnp.exp(mm - mn)
                    run_s[pr, dst, :] = mn
                    run_s[2 + pr, dst, :] = a * run_s[2 + pr, dst, :] + b * ll
                    run_s[4 + pr, dst, :] = a * run_s[4 + pr, dst, :] + b * acc
            return carry

        lax.fori_loop(0, SEQ // QB, block, 0, unroll=8)

    for c in range(n_chunks):
        rows = slice(c * TS, (c + 1) * TS)
        attn_ref[rows, :] = jnp.concatenate([run_s[4, rows] / run_s[2, rows], run_s[5, rows] / run_s[3, rows]],
                                            axis=1).astype(BF16)


def _attn_prompt(x, g_pre, w_qkv_all, cos_t, sin_t):
    B = x.shape[0]
    W = GROUP_WIDTH
    keeps = [min(w, SEQ) for w, _ in ATTN_GROUPS]
    one = pl.Buffered(1)
    const = lambda shape: pl.BlockSpec(shape, lambda b: (0,) * len(shape), pipeline_mode=one)
    kv_specs, kv_shapes = [], []
    for keep in keeps:
        kv_specs += [pl.BlockSpec((None, W, keep), lambda b: (b, 0, 0))] * 2
        kv_shapes += [jax.ShapeDtypeStruct((B, W, keep), F32)] * 2
    outs = pl.pallas_call(
        _attn_prompt_kernel,
        grid=(B,),
        in_specs=[pl.BlockSpec((None, SEQ, D_MODEL), lambda b: (b, 0, 0)),
                  const((1, D_MODEL)), const((D_MODEL, 9 * W)), const((SEQ, W)), const((SEQ, W))],
        out_specs=[pl.BlockSpec((None, SEQ, W), lambda b: (b, 0, 0))] + kv_specs,
        out_shape=[jax.ShapeDtypeStruct((B, SEQ, W), BF16)] + kv_shapes,
        scratch_shapes=[pltpu.VMEM((SEQ, D_MODEL), BF16),
                        pltpu.VMEM((6, SEQ, LANES), F32),
                        pltpu.VMEM((6, SEQ, LANES), F32)],
        compiler_params=pltpu.CompilerParams(dimension_semantics=("arbitrary",),
                                             vmem_limit_bytes=VMEM_LIMIT_V7X),
        name="attn_prompt",
    )(x, g_pre, w_qkv_all, cos_t, sin_t)
    kv = [a.reshape(B, HEADS, HEAD_DIM, a.shape[2]).transpose(0, 3, 1, 2)[None] for a in outs[1:]]
    return outs[0], kv


def _gelu_tanh(x):
    c = math.sqrt(2.0 / math.pi)
    u = x * (2.0 * c + (2.0 * c * 0.044715) * (x * x))
    return x / (1.0 + jnp.exp(-u))


def _dense_body(x, attn, wr, upbuf, actbuf, *, roll_steps, shift_in, u_hist, cnt_of, keep_u, up_hist, keep_up):
    (gpre, gpost, gfpre, gfpost, wpool, wgate, bgate, poolw, pscale, wbp, wba, wout,
     wupa, wupb, cwa, cwb, cba, cbb, wdown) = wr
    R = x.shape[0]
    h = _rms(x, gpre[...]).astype(BF16)

    u = jnp.dot(h, wpool[...], preferred_element_type=F32)
    keep_u(u)
    hist = u_hist()
    H = hist.shape[0]
    s = jnp.concatenate([hist, u], axis=0)
    ys = []
    for gi, w in enumerate(POOL_WINDOWS):
        c0 = gi * POOL_GROUP
        s = s[:, POOL_GROUP:] if gi else s
        s = s + roll_steps(s, w // 2)
        zmix = s[H:, :POOL_GROUP] / cnt_of(w) - u[:, c0:c0 + POOL_GROUP]
        ys.append(jnp.dot(zmix.astype(BF16), poolw[gi], preferred_element_type=F32))
    pool_out = jnp.concatenate(ys, axis=1) * pscale[...]
    a_br = jnp.dot(pool_out.astype(BF16), wbp[...], preferred_element_type=F32)

    b_br = jnp.dot(attn.astype(BF16), wba[...], preferred_element_type=F32)

    gates = jax.nn.sigmoid(jnp.dot(h, wgate[...], preferred_element_type=F32) + bgate[...])
    merged = gates[:, :D_MODEL] * a_br + gates[:, D_MODEL:] * b_br
    mix = jnp.dot(merged.astype(BF16), wout[...], preferred_element_type=F32)
    x1 = x + _rms(mix, gpost[...])

    hn = _rms(x1, gfpre[...]).astype(BF16)
    acc = jnp.zeros((R, D_MODEL), F32)

    def project(c, half):
        w_up = (wupa, wupb)[half]
        upbuf[c % N_BUF, half] = jnp.dot(hn, w_up[c], preferred_element_type=F32)

    def conv_half(c, half, up):
        w = (cwa, cwb)[half][c]
        hist_c = up_hist(c, half)
        y = (cba, cbb)[half][c] + up * w[CONV_W - 1:CONV_W]
        cur = up
        for j in range(1, CONV_W):
            cur = shift_in(cur, hist_c, j)
            y = y + cur * w[CONV_W - 1 - j:CONV_W - j]
        return y

    project(0, 0)
    project(0, 1)
    for c in range(NCH + 1):
        last = c == NCH
        if c + 1 < NCH:
            project(c + 1, 0)
        if not last:
            ups = (upbuf[c % N_BUF, 0], upbuf[c % N_BUF, 1])
            keep_up(c, *ups)
            ya = conv_half(c, 0, ups[0])
        if c + 1 < NCH:
            project(c + 1, 1)
        if not last:
            yb = conv_half(c, 1, ups[1])
        if c >= 1:
            acc = acc + jnp.dot(actbuf[(c - 1) % N_BUF], wdown[c - 1], preferred_element_type=F32)
        if not last:
            actbuf[c % N_BUF] = (_gelu_tanh(ya) * yb).astype(BF16)
    return x1 + _rms(acc, gfpost[...])


N_W = 19
HU_ROWS = 16
HC_ROWS = 8


def _dense_prompt_kernel(*refs):
    x_ref, attn_ref = refs[0:2]
    wr = refs[2:2 + N_W]
    y_ref, pool_ref, conv_ref = refs[2 + N_W:5 + N_W]
    uhist, chist, upbuf, actbuf = refs[5 + N_W:]
    i = pl.program_id(1)

    @pl.when(i == 0)
    def _():
        uhist[...] = jnp.zeros(uhist.shape, F32)
        chist[...] = jnp.zeros(chist.shape, F32)

    pos = i * TS + lax.broadcasted_iota(jnp.int32, (TS, 1), 0)
    hist_u = uhist[...]

    def keep_u(u):
        uhist[...] = u[TS - HU_ROWS:]
        pool_ref[...] = u[TS - POOL_HIST:]

    def keep_up(c, ua, ub):
        for half, up in enumerate((ua, ub)):
            conv_ref[:, half * D_FF + c * FC:half * D_FF + (c + 1) * FC] = up[TS - (CONV_W - 1):]

    def up_hist(c, half):
        return chist[2 * c + half]

    def roll_steps(a, j):
        return pltpu.roll(a, j, 0)

    row8 = lax.broadcasted_iota(jnp.int32, (SUBLANES, FC), 0)

    def shift_in(a, hist, j):
        r = pltpu.roll(a, 1, 0)
        top = jnp.where(row8 == 0, hist[HC_ROWS - j:HC_ROWS - j + 1], r[0:SUBLANES])
        return jnp.concatenate([top, r[SUBLANES:]], axis=0)

    x = x_ref[...]
    wr_hist = []

    def keep_up_and_hist(c, ua, ub):
        keep_up(c, ua, ub)
        wr_hist.append((c, ua[TS - HC_ROWS:], ub[TS - HC_ROWS:]))

    y_ref[...] = _dense_body(x, attn_ref[...], wr, upbuf, actbuf, roll_steps=roll_steps, shift_in=shift_in, u_hist=lambda: hist_u,
                             cnt_of=lambda w: jnp.minimum(pos + 1, w).astype(F32),
                             keep_u=keep_u, up_hist=up_hist, keep_up=keep_up_and_hist)
    for c, ta, tb in wr_hist:
        chist[2 * c] = ta
        chist[2 * c + 1] = tb


def _dense_sample_kernel(*refs, bs):
    x_ref, attn_ref, ph_ref, ch_ref = refs[0:4]
    wr = refs[4:4 + N_W]
    y_ref, u_ref, conv_ref = refs[4 + N_W:7 + N_W]
    upbuf, actbuf = refs[7 + N_W:]
    R = DEC_SEQ * bs
    HC = (CONV_W - 1) * bs

    def keep_u(u):
        u_ref[...] = u

    def keep_up(c, ua, ub):
        conv_ref[:, c * FC:(c + 1) * FC] = ua[R - HC:]
        conv_ref[:, D_FF + c * FC:D_FF + (c + 1) * FC] = ub[R - HC:]

    def up_hist(c, half):
        c0 = half * D_FF + c * FC
        return ch_ref[:, c0:c0 + FC]

    def roll_steps(a, j):
        return jnp.concatenate([a[a.shape[0] - j * bs:], a[:a.shape[0] - j * bs]], axis=0)

    def shift_in(a, hist, j):
        return jnp.concatenate([hist[HC - j * bs:HC - (j - 1) * bs], a[:R - bs]], axis=0)

    y_ref[...] = _dense_body(x_ref[...], attn_ref[...], wr, upbuf, actbuf, roll_steps=roll_steps, shift_in=shift_in, u_hist=lambda: ph_ref[...],
                             cnt_of=float, keep_u=keep_u, up_hist=up_hist, keep_up=keep_up)


def _weight_shapes():
    return [
        (1, D_MODEL), (1, D_MODEL), (1, D_MODEL), (1, D_MODEL),
        (D_MODEL, POOL_WIDTH), (D_MODEL, 2 * D_MODEL), (1, 2 * D_MODEL),
        (len(POOL_WINDOWS), POOL_GROUP, POOL_GROUP), (1, POOL_WIDTH),
        (POOL_WIDTH, D_MODEL), (GROUP_WIDTH, D_MODEL), (D_MODEL, D_MODEL),
        (NCH, D_MODEL, FC), (NCH, D_MODEL, FC),
        (NCH, CONV_W, FC), (NCH, CONV_W, FC),
        (NCH, 1, FC), (NCH, 1, FC),
        (NCH, FC, D_MODEL),
    ]


def _dense_prompt(x, attn, weights):
    B = x.shape[0]
    steps = SEQ // TS
    tile = lambda w: pl.BlockSpec((None, TS, w), lambda b, i: (b, i, 0))
    wspecs = [pl.BlockSpec(s, lambda b, i, n=len(s): (0,) * n, pipeline_mode=pl.Buffered(1)) for s in _weight_shapes()]
    y, pool_p, conv_p = pl.pallas_call(
        _dense_prompt_kernel,
        grid=(B, steps),
        in_specs=[tile(D_MODEL), tile(GROUP_WIDTH)] + wspecs,
        out_specs=[tile(D_MODEL),
                   pl.BlockSpec((None, POOL_HIST, POOL_WIDTH), lambda b, i: (b, 0, 0)),
                   pl.BlockSpec((None, CONV_W - 1, 2 * D_FF), lambda b, i: (b, 0, 0))],
        out_shape=[jax.ShapeDtypeStruct((B, SEQ, D_MODEL), F32),
                   jax.ShapeDtypeStruct((B, POOL_HIST, POOL_WIDTH), F32),
                   jax.ShapeDtypeStruct((B, CONV_W - 1, 2 * D_FF), F32)],
        scratch_shapes=[pltpu.VMEM((HU_ROWS, POOL_WIDTH), F32),
                        pltpu.VMEM((2 * NCH, HC_ROWS, FC), F32),
                        pltpu.VMEM((N_BUF, 2, TS, FC), F32),
                        pltpu.VMEM((N_BUF, TS, FC), BF16)],
        compiler_params=pltpu.CompilerParams(dimension_semantics=("arbitrary", "arbitrary"),
                                             vmem_limit_bytes=VMEM_LIMIT_V7X),
        name="dense_prompt",
    )(x, attn, *weights)
    return y, pool_p, conv_p


def _dense_sample(x_tm, attn_tm, pool_hist_tm, conv_hist_tm, weights, bs):
    R = DEC_SEQ * bs
    full = lambda shape: pl.BlockSpec(shape, lambda n=len(shape): (0,) * n)
    y, u, conv_s = pl.pallas_call(
        functools.partial(_dense_sample_kernel, bs=bs),
        in_specs=[full((R, D_MODEL)), full((R, GROUP_WIDTH)),
                  full((POOL_HIST * bs, POOL_WIDTH)), full(((CONV_W - 1) * bs, 2 * D_FF))]
                 + [full(s) for s in _weight_shapes()],
        out_specs=[full((R, D_MODEL)), full((R, POOL_WIDTH)), full(((CONV_W - 1) * bs, 2 * D_FF))],
        out_shape=[jax.ShapeDtypeStruct((R, D_MODEL), F32),
                   jax.ShapeDtypeStruct((R, POOL_WIDTH), F32),
                   jax.ShapeDtypeStruct(((CONV_W - 1) * bs, 2 * D_FF), F32)],
        scratch_shapes=[pltpu.VMEM((N_BUF, 2, R, FC), F32), pltpu.VMEM((N_BUF, R, FC), BF16)],
        compiler_params=pltpu.CompilerParams(vmem_limit_bytes=VMEM_LIMIT_V7X),
        name="dense_sample",
    )(x_tm, attn_tm, pool_hist_tm, conv_hist_tm, *weights)
    return y, u, conv_s


def _sample_qkv_kernel(x_ref, g_ref, w_ref, cos_ref, sin_ref, q_ref, k_ref, v_ref):
    W = GROUP_WIDTH
    h = _rms(x_ref[...], g_ref[...]).astype(BF16)
    qkv = jnp.dot(h, w_ref[...], preferred_element_type=F32)
    cos = cos_ref[...]
    sin = sin_ref[...]
    for g in range(len(ATTN_GROUPS)):
        b0 = 3 * W * g
        q_ref[:, g * W:(g + 1) * W] = _rope(qkv[:, b0:b0 + W], cos, sin) * QK_SCALE
        k_ref[:, g * W:(g + 1) * W] = _rope(qkv[:, b0 + W:b0 + 2 * W], cos, sin)
        v_ref[:, g * W:(g + 1) * W] = qkv[:, b0 + 2 * W:b0 + 3 * W]


def _sample_qkv(x_tm, g_pre, w_qkv_all, cos_s, sin_s):
    R = x_tm.shape[0]
    W = GROUP_WIDTH
    full = lambda shape: pl.BlockSpec(shape, lambda n=len(shape): (0,) * n)
    return pl.pallas_call(
        _sample_qkv_kernel,
        in_specs=[full((R, D_MODEL)), full((1, D_MODEL)), full((D_MODEL, 9 * W)), full((R, W)), full((R, W))],
        out_specs=[full((R, 3 * W))] * 3,
        out_shape=[jax.ShapeDtypeStruct((R, 3 * W), F32)] * 3,
        compiler_params=pltpu.CompilerParams(vmem_limit_bytes=VMEM_LIMIT_V7X),
        name="sample_qkv",
    )(x_tm, g_pre, w_qkv_all, cos_s, sin_s)


Q_ROWS = 16


def _lane_sum(a):
    return jnp.sum(a, axis=1, keepdims=True)


def _fold_rows(a):
    out = a[0:SUBLANES]
    for i in range(1, a.shape[0] // SUBLANES):
        out = out + a[i * SUBLANES:(i + 1) * SUBLANES]
    return out


def _sample_attn_roll_kernel(*refs, bb):
    q_ref, new_ref = refs[0:2]
    caches = refs[2:8]
    attn_ref = refs[8]
    outs = refs[9:15]
    news_s = refs[15]
    T = DEC_SEQ
    W = GROUP_WIDTH
    NEW0 = LANES - T

    t_row = lax.broadcasted_iota(jnp.int32, (Q_ROWS, LANES), 0)
    lane_q = lax.broadcasted_iota(jnp.int32, (Q_ROWS, LANES), 1)
    pat_new = jnp.where((lane_q == NEW0 + t_row) & (t_row < T), 1.0, 0.0).astype(BF16)
    for j in range(bb):
        x = new_ref[j]
        hi = x.astype(BF16)
        r1 = x - hi.astype(F32)
        mid = r1.astype(BF16)
        lo = (r1 - mid.astype(F32)).astype(BF16)
        for a in range(len(caches)):
            cols = slice(a * W, (a + 1) * W)
            tile = sum(lax.dot_general(part[:, cols], pat_new, (((0,), (0,)), ((), ())), preferred_element_type=F32)
                       for part in (hi, mid, lo))
            news_s[a, j * W:(j + 1) * W, :] = tile
    news = [news_s.at[a] for a in range(len(caches))]

    lane_blk = lax.broadcasted_iota(jnp.int32, (bb * W, LANES), 1)
    for c_ref, n_ref, o_ref in zip(caches, news, outs):
        wb = c_ref.shape[-1]
        rolled = pltpu.roll(c_ref[...], wb - T, 1)
        if wb > LANES:
            o_ref[:, :wb - LANES] = rolled[:, :wb - LANES]
        o_ref[:, wb - LANES:] = jnp.where(lane_blk >= NEW0, n_ref[...], rolled[:, wb - LANES:])

    lane1 = lax.broadcasted_iota(jnp.int32, (1, LANES), 1)
    lane_o = lax.broadcasted_iota(jnp.int32, (W, LANES), 1)
    for j in range(bb):
        q16 = q_ref[j].astype(BF16)
        per_group = []
        for g, (_, dil) in enumerate(ATTN_GROUPS):
            kc, vc, kn, vn = caches[2 * g], caches[2 * g + 1], news[2 * g], news[2 * g + 1]
            wb = kc.shape[-1]
            nt = wb // LANES
            qg = q16[:, g * W:(g + 1) * W]
            if dil == 1:
                pat = jnp.concatenate([jnp.where(t_row == t, 1.0, 0.0) for t in range(T)], axis=1)
            else:
                pat = jnp.concatenate([jnp.where((lane_q % dil == t_row) & (t_row < T), 1.0, 0.0),
                                       jnp.where((lane_q == NEW0 + t_row) & (t_row < T), 1.0, 0.0)], axis=1)
            qpat = lax.dot_general(qg, pat.astype(BF16), (((0,), (0,)), ((), ())),
                                   preferred_element_type=F32)
            res = [[None] * HEADS for _ in range(T)]
            hrs = [slice(hh * HEAD_DIM, (hh + 1) * HEAD_DIM) for hh in range(HEADS)]
            rws = [slice(j * W + hh * HEAD_DIM, j * W + (hh + 1) * HEAD_DIM) for hh in range(HEADS)]

            def heads_on_rows(per_head):
                n = per_head[0].shape[1]
                row = lax.broadcasted_iota(jnp.int32, (SUBLANES, n), 0)
                out = jnp.full((SUBLANES, n), NEG, F32)
                for hh in range(HEADS):
                    out = jnp.where(row == hh, per_head[hh], out)
                return out

            if dil == 1:
                for t in range(T):
                    qps = [qpat[hrs[hh], t * LANES:(t + 1) * LANES] for hh in range(HEADS)]
                    s = heads_on_rows([jnp.sum(_fold_rows(kc[rws[hh], :] * qps[hh]), axis=0, keepdims=True)
                                       for hh in range(HEADS)])
                    sn = heads_on_rows([jnp.sum(_fold_rows(kn[rws[hh], :] * qps[hh]), axis=0, keepdims=True)
                                        for hh in range(HEADS)])
                    ok_c = lane1 >= t
                    ok_n = (lane1 >= NEW0) & (lane1 <= NEW0 + t)
                    m = jnp.maximum(jnp.max(jnp.where(ok_c, s, NEG), axis=1, keepdims=True),
                                    jnp.max(jnp.where(ok_n, sn, NEG), axis=1, keepdims=True))
                    p = jnp.exp(jnp.where(ok_c, s - m, NEG))
                    pn = jnp.exp(jnp.where(ok_n, sn - m, NEG))
                    l = _lane_sum(p) + _lane_sum(pn)
                    for hh in range(HEADS):
                        num = (_lane_sum(vc[rws[hh], :] * jnp.broadcast_to(p[hh:hh + 1], (HEAD_DIM, LANES)))
                               + _lane_sum(vn[rws[hh], :] * jnp.broadcast_to(pn[hh:hh + 1], (HEAD_DIM, LANES))))
                        res[t][hh] = (m[hh:hh + 1], l[hh:hh + 1], num)
            else:
                s = heads_on_rows([jnp.sum(jnp.concatenate(
                    [_fold_rows(kc[rws[hh], i * LANES:(i + 1) * LANES] * qpat[hrs[hh], :LANES]) for i in range(nt)],
                    axis=1), axis=0, keepdims=True) for hh in range(HEADS)])
                sn = heads_on_rows([jnp.sum(_fold_rows(kn[rws[hh], :] * qpat[hrs[hh], LANES:]), axis=0, keepdims=True)
                                    for hh in range(HEADS)])
                cls = lax.broadcasted_iota(jnp.int32, (1, wb), 1) % dil
                ms = []
                for t in range(T):
                    ms.append(jnp.maximum(jnp.max(jnp.where(cls == t, s, NEG), axis=1, keepdims=True),
                                          sn[:, NEW0 + t:NEW0 + t + 1]))
                m_lane = jnp.where(cls == 0, ms[0], jnp.where(cls == 1, ms[1], jnp.where(cls == 2, ms[2], ms[3])))
                p = jnp.exp(jnp.where(cls < T, s - m_lane, NEG))
                pns = [jnp.exp(sn[:, NEW0 + t:NEW0 + t + 1] - ms[t]) for t in range(T)]
                ls = [_lane_sum(jnp.where(cls == t, p, 0.0)) + pns[t] for t in range(T)]
                cls1 = lane1 % dil
                for hh in range(HEADS):
                    pb = jnp.broadcast_to(p[hh:hh + 1], (HEAD_DIM, wb))
                    acc = vc[rws[hh], 0:LANES] * pb[:, 0:LANES]
                    for i in range(1, nt):
                        acc = acc + vc[rws[hh], i * LANES:(i + 1) * LANES] * pb[:, i * LANES:(i + 1) * LANES]
                    for t in range(T):
                        num = (_lane_sum(jnp.where(cls1 == t, acc, 0.0))
                               + vn[rws[hh], NEW0 + t:NEW0 + t + 1] * pns[t][hh:hh + 1])
                        res[t][hh] = (ms[t][hh:hh + 1], ls[t][hh:hh + 1], num)
            per_group.append(res)
        tile = jnp.zeros((W, LANES), F32)
        for t in range(T):
            cols = []
            for hh in range(HEADS):
                trip = [per_group[g][t][hh] for g in range(len(ATTN_GROUPS))]
                mx = jnp.maximum(jnp.maximum(trip[0][0], trip[1][0]), trip[2][0])
                den = 0.0
                num = 0.0
                for m, l, nm in trip:
                    e = jnp.exp(m - mx)
                    den = den + e * l
                    num = num + nm * e
                cols.append(num / den)
            tile = jnp.where(lane_o == t, jnp.concatenate(cols, axis=0), tile)
        attn_ref[j] = tile.T[0:SUBLANES]


def _sample_attn_roll(q_b, new_b, caches_t, bs, bb):
    W = GROUP_WIDTH
    n = len(caches_t)
    cspecs = [pl.BlockSpec((bb * W, c.shape[1]), lambda i: (i, 0)) for c in caches_t]
    outs = pl.pallas_call(
        functools.partial(_sample_attn_roll_kernel, bb=bb),
        grid=(bs // bb,),
        in_specs=[pl.BlockSpec((bb, Q_ROWS, 3 * W), lambda i: (i, 0, 0)),
                  pl.BlockSpec((bb, Q_ROWS, n * W), lambda i: (i, 0, 0))] + cspecs,
        out_specs=[pl.BlockSpec((bb, SUBLANES, W), lambda i: (i, 0, 0))] + cspecs,
        out_shape=[jax.ShapeDtypeStruct((bs, SUBLANES, W), F32)]
                  + [jax.ShapeDtypeStruct(c.shape, c.dtype) for c in caches_t],
        scratch_shapes=[pltpu.VMEM((n, bb * W, LANES), F32)],
        compiler_params=pltpu.CompilerParams(dimension_semantics=("arbitrary",),
                                             vmem_limit_bytes=VMEM_LIMIT_V7X),
        name="sample_attn_roll",
    )(q_b, new_b, *caches_t)
    return outs[0], outs[1:]


def _rope_tables(pos):
    inv_freq = ROPE_THETA ** (-jnp.arange(0, HEAD_DIM, 2, dtype=F32) / HEAD_DIM)
    ang = pos.astype(F32)[:, None] * inv_freq[None, :]
    cos = jnp.cos(ang)
    sin = jnp.sin(ang)
    cos_t = jnp.tile(jnp.concatenate([cos, cos], axis=-1), (1, HEADS))
    sin_t = jnp.tile(jnp.concatenate([-sin, sin], axis=-1), (1, HEADS))
    return cos_t, sin_t


def _layer_weights(l, norm_mix_pre, norm_mix_post, norm_ffn_pre, norm_ffn_post, w_in, b_gate, pool_w, pool_scale,
                   w_branch_pool, w_branch_attn, w_out, ffn_w_up, ffn_conv_w, ffn_conv_b, ffn_w_down):
    row = lambda a: a[l].reshape(1, -1)
    chunk_cols = lambda a: a.reshape(a.shape[0], NCH, FC).transpose(1, 0, 2)
    wi = w_in[l]
    up = ffn_w_up[l]
    cw = ffn_conv_w[l]
    cb = ffn_conv_b[l]
    dense = (
        row(norm_mix_pre), row(norm_mix_post), row(norm_ffn_pre), row(norm_ffn_post),
        wi[:, :POOL_WIDTH].astype(BF16), wi[:, GATE_OFF:].astype(BF16), row(b_gate),
        pool_w[l].astype(BF16), row(pool_scale),
        w_branch_pool[l].astype(BF16), w_branch_attn[l].astype(BF16), w_out[l].astype(BF16),
        chunk_cols(up[:, :D_FF]).astype(BF16), chunk_cols(up[:, D_FF:]).astype(BF16),
        chunk_cols(cw[:, :D_FF]), chunk_cols(cw[:, D_FF:]),
        cb[:D_FF].reshape(NCH, 1, FC), cb[D_FF:].reshape(NCH, 1, FC),
        ffn_w_down[l].reshape(NCH, FC, D_MODEL).astype(BF16),
    )
    w_qkv_all = wi[:, POOL_WIDTH:GATE_OFF].astype(BF16)
    return dense, w_qkv_all, row(norm_mix_pre)


def kernel(x_prompt, x_sample, state_pool, cache_k_w128, cache_v_w128, cache_k_w512, cache_v_w512, cache_k_w2048, cache_v_w2048, state_ffn_conv, norm_mix_pre, norm_mix_post, norm_ffn_pre, norm_ffn_post, w_in, b_gate, pool_w, pool_scale, w_branch_pool, w_branch_attn, w_out, ffn_w_up, ffn_conv_w, ffn_conv_b, ffn_w_down):
    depth = w_in.shape[0]
    assert depth == 1, "single-layer trunk"
    assert x_prompt.shape[1:] == (SEQ, D_MODEL) and x_sample.shape[1:] == (DEC_SEQ, D_MODEL)
    l = 0
    bs = x_sample.shape[0]
    W = GROUP_WIDTH
    dense_w, w_qkv_all, g_pre = _layer_weights(
        l, norm_mix_pre, norm_mix_post, norm_ffn_pre, norm_ffn_post, w_in, b_gate, pool_w, pool_scale,
        w_branch_pool, w_branch_attn, w_out, ffn_w_up, ffn_conv_w, ffn_conv_b, ffn_w_down)

    cos_p, sin_p = _rope_tables(jnp.arange(SEQ, dtype=jnp.int32))
    attn_p, kv_p = _attn_prompt(x_prompt, g_pre, w_qkv_all, cos_p, sin_p)
    y_p, pool_p, conv_p = _dense_prompt(x_prompt, attn_p, dense_w)

    cos_s, sin_s = _rope_tables(PAST_LEN + jnp.arange(DEC_SEQ, dtype=jnp.int32))
    cos_s = jnp.repeat(cos_s, bs, axis=0)
    sin_s = jnp.repeat(sin_s, bs, axis=0)
    x_tm = x_sample.transpose(1, 0, 2).reshape(DEC_SEQ * bs, D_MODEL)
    q_tm, k_tm, v_tm = _sample_qkv(x_tm, g_pre, w_qkv_all, cos_s, sin_s)
    by_batch = lambda a: jnp.pad(a.reshape(DEC_SEQ, bs, a.shape[1]).transpose(1, 0, 2),
                                 ((0, 0), (0, Q_ROWS - DEC_SEQ), (0, 0)))
    q_b = by_batch(q_tm)
    new_b = by_batch(jnp.concatenate(
        [a[:, g * W:(g + 1) * W] for g in range(len(ATTN_GROUPS)) for a in (k_tm, v_tm)], axis=1))
    cache_in = (cache_k_w128, cache_v_w128, cache_k_w512, cache_v_w512, cache_k_w2048, cache_v_w2048)
    caches_t = [c[l].transpose(0, 2, 3, 1).reshape(bs * W, c.shape[2]) for c in cache_in]
    attn_b, rolled = _sample_attn_roll(q_b, new_b, caches_t, bs, bb=2)
    attn_s = attn_b[:, :DEC_SEQ].transpose(1, 0, 2).reshape(DEC_SEQ * bs, W)
    kv_s = [r.reshape(bs, HEADS, HEAD_DIM, r.shape[1]).transpose(0, 3, 1, 2)[None] for r in rolled]
    pool_hist_tm = state_pool[l].transpose(1, 0, 2).reshape(POOL_HIST * bs, POOL_WIDTH)
    conv_hist_tm = state_ffn_conv[l].transpose(1, 0, 2).reshape((CONV_W - 1) * bs, 2 * D_FF)
    y_tm, u_tm, conv_tm = _dense_sample(x_tm, attn_s, pool_hist_tm, conv_hist_tm, dense_w, bs)
    y_s = y_tm.reshape(DEC_SEQ, bs, D_MODEL).transpose(1, 0, 2)
    u_s = u_tm.reshape(DEC_SEQ, bs, POOL_WIDTH).transpose(1, 0, 2)
    pool_s = jnp.concatenate([state_pool[l][:, DEC_SEQ:], u_s], axis=1)
    conv_s = conv_tm.reshape(CONV_W - 1, bs, 2 * D_FF).transpose(1, 0, 2)

    return (y_p, y_s, pool_p[None], pool_s[None],
            kv_p[0], kv_s[0], kv_p[1], kv_s[1], kv_p[2], kv_s[2], kv_p[3], kv_s[3],
            kv_p[4], kv_s[4], kv_p[5], kv_s[5], conv_p[None], conv_s[None])
```
